```python
import jax, jax.numpy as jnp
from jax import lax
import numpy as np

D_MODEL = 1024
BATCH = 4
SEQ = 8192
DEPTH = 2

N_MIXERS = 2
EPS = 1e-6

A_HEADS = 16
A_KV_HEADS = 4
A_HEAD_DIM = D_MODEL // A_HEADS
A_WIDTH = A_HEADS * A_HEAD_DIM
A_KV_WIDTH = A_KV_HEADS * A_HEAD_DIM
WINDOW = 128
BLOCK = WINDOW
A_IN = 2 * A_WIDTH + 2 * A_KV_WIDTH

B_HEADS = 8
B_HEAD_K = 128
B_HEAD_V = 128
B_KW = B_HEADS * B_HEAD_K
B_VW = B_HEADS * B_HEAD_V
CONV_WIDTH = 5
CHUNK = 64
B_IN = 2 * B_KW + 2 * B_VW + 4 * B_HEADS

kernel_name = "hybrid_swa_sink_alibi_gated_deltanet_bidir"


def rms_norm(x, g):
    xf = x.astype(jnp.float32)
    y = xf * lax.rsqrt(jnp.mean(xf * xf, axis=-1, keepdims=True) + EPS)
    return (y * g.astype(jnp.float32)).astype(x.dtype)


def l2_norm(x):
    return x * lax.rsqrt(jnp.sum(x * x, axis=-1, keepdims=True) + EPS)


def alibi_slopes(n_heads):
    return jnp.asarray(np.power(2.0, -8.0 * np.arange(1, n_heads + 1) / n_heads), dtype=jnp.float32)


def windowed_gqa_sink(q, k, v, sink):
    b, s, _ = q.shape
    g, r, dh = A_KV_HEADS, A_HEADS // A_KV_HEADS, A_HEAD_DIM
    nb = s // BLOCK
    f32 = jnp.float32
    qb = q.astype(f32).reshape(b, nb, BLOCK, g, r, dh)

    def band(t):
        tp = jnp.pad(t.astype(f32).reshape(b, s, g, dh), ((0, 0), (BLOCK, BLOCK), (0, 0), (0, 0)))
        tp = tp.reshape(b, nb + 2, BLOCK, g, dh)
        return jnp.concatenate([tp[:, :-2], tp[:, 1:-1], tp[:, 2:]], axis=2)

    kb, vb = band(k), band(v)
    qpos = jnp.arange(BLOCK)[:, None]
    krel = jnp.arange(3 * BLOCK)[None, :] - BLOCK
    dist = jnp.abs(krel - qpos)
    kabs = jnp.arange(nb)[:, None] * BLOCK + krel
    valid = (dist <= WINDOW)[None] & ((kabs >= 0) & (kabs < s))[:, None, :]
    slopes = alibi_slopes(A_HEADS).reshape(g, r)
    alibi = -slopes[:, :, None, None] * dist.astype(f32)
    bias = jnp.where(valid[:, None, None], alibi[None], -jnp.inf)

    scores = jnp.einsum('bnqgrd,bnkgd->bngrqk', qb, kb) * (dh ** -0.5) + bias[None]
    sk = sink.astype(f32).reshape(g, r)[None, None, :, :, None]
    m = jnp.maximum(jnp.max(scores, axis=-1), sk)
    p = jnp.exp(scores - m[..., None])
    denom = jnp.sum(p, axis=-1) + jnp.exp(sk - m)
    o = jnp.einsum('bngrqk,bnkgd->bnqgrd', p, vb)
    o = o / jnp.transpose(denom, (0, 1, 4, 2, 3))[..., None]
    return o.reshape(b, s, g * r * dh)


def short_conv_centred(x, w):
    pad = w.shape[0] // 2
    return lax.conv_general_dilated(
        x, w[:, None, :].astype(x.dtype), window_strides=(1,), padding=[(pad, pad)],
        dimension_numbers=('NWC', 'WIO', 'NWC'), feature_group_count=x.shape[-1])


def gated_delta_chunked(q, k, v, g, beta):
    b, s, h, dk = q.shape
    dv = v.shape[-1]
    n = s // CHUNK

    def chunks(t):
        return jnp.moveaxis(t.reshape(b, n, CHUNK, h, *t.shape[3:]), 3, 2)

    q = chunks(q) * (dk ** -0.5)
    k, v, g, beta = chunks(k), chunks(v), chunks(g), chunks(beta)
    gc = jnp.cumsum(g, axis=-1)
    tril = jnp.tril(jnp.ones((CHUNK, CHUNK), dtype=bool))
    strict = jnp.tril(jnp.ones((CHUNK, CHUNK), dtype=bool), -1)
    diff = gc[..., :, None] - gc[..., None, :]
    decay = jnp.where(tril, jnp.exp(jnp.where(tril, diff, 0.0)), 0.0)

    kb = k * beta[..., None]
    lower = jnp.where(strict, jnp.einsum('bnhcd,bnhed->bnhce', kb, k) * decay, 0.0)
    a_mat = jnp.eye(CHUNK, dtype=q.dtype) + lower
    rhs = jnp.concatenate([v * beta[..., None], kb * jnp.exp(gc)[..., None]], axis=-1)
    sol = lax.linalg.triangular_solve(a_mat, rhs, left_side=True, lower=True, unit_diagonal=True)
    u, w = sol[..., :dv], sol[..., dv:]

    qk = jnp.where(tril, jnp.einsum('bnhcd,bnhed->bnhce', q, k) * decay, 0.0)
    g_last = gc[..., -1]
    k_dec = k * jnp.exp(g_last[..., None] - gc)[..., None]
    q_dec = q * jnp.exp(gc)[..., None]

    def step(state, xs):
        q_i, qk_i, u_i, w_i, k_i, gl_i = xs
        v_new = u_i - jnp.einsum('bhck,bhkv->bhcv', w_i, state)
        o_i = jnp.einsum('bhck,bhkv->bhcv', q_i, state) + jnp.einsum('bhce,bhev->bhcv', qk_i, v_new)
        state = state * jnp.exp(gl_i)[..., None, None] + jnp.einsum('bhck,bhcv->bhkv', k_i, v_new)
        return state, o_i

    xs = tuple(jnp.moveaxis(t, 1, 0) for t in (q_dec, qk, u, w, k_dec, g_last))
    s0 = jnp.zeros((b, h, dk, dv), dtype=q.dtype)
    _, o = lax.scan(step, s0, xs)
    return jnp.transpose(o, (1, 0, 3, 2, 4)).reshape(b, s, h, dv)


def attention_mixer(hn, w_in, sink, w_out):
    proj = hn @ w_in
    q = proj[..., :A_WIDTH]
    k = proj[..., A_WIDTH:A_WIDTH + A_KV_WIDTH]
    v = proj[..., A_WIDTH + A_KV_WIDTH:A_WIDTH + 2 * A_KV_WIDTH]
    gate = proj[..., A_WIDTH + 2 * A_KV_WIDTH:]
    o = windowed_gqa_sink(q, k, v, sink)
    o = (o * jax.nn.silu(gate.astype(jnp.float32))).astype(hn.dtype)
    return o @ w_out


def deltanet_mixer(hn, w_in, conv_w, a_log, dt_bias, out_norm, w_out):
    b, s, _ = hn.shape
    f32 = jnp.float32
    proj = hn @ w_in
    n_qkv = 2 * B_KW + B_VW
    qkv = jax.nn.silu(short_conv_centred(proj[..., :n_qkv], conv_w)).astype(f32)
    gate = proj[..., n_qkv:n_qkv + B_VW].astype(f32)
    ba = proj[..., n_qkv + B_VW:].astype(f32)
    q = l2_norm(qkv[..., :B_KW].reshape(b, s, B_HEADS, B_HEAD_K))
    k = l2_norm(qkv[..., B_KW:2 * B_KW].reshape(b, s, B_HEADS, B_HEAD_K))
    v = qkv[..., 2 * B_KW:].reshape(b, s, B_HEADS, B_HEAD_V)
    beta = jax.nn.sigmoid(ba[..., :2 * B_HEADS]).reshape(b, s, 2, B_HEADS)
    a = ba[..., 2 * B_HEADS:].reshape(b, s, 2, B_HEADS)
    g = -jnp.exp(a_log.astype(f32)) * jax.nn.softplus(a + dt_bias.astype(f32))

    flip = lambda t: jnp.flip(t, axis=1)
    o_fwd = gated_delta_chunked(q, k, v, g[:, :, 0], beta[:, :, 0])
    o_bwd = flip(gated_delta_chunked(flip(q), flip(k), flip(v), flip(g[:, :, 1]), flip(beta[:, :, 1])))
    o = o_fwd + o_bwd
    o = o * lax.rsqrt(jnp.mean(o * o, axis=-1, keepdims=True) + EPS) * out_norm.astype(f32)
    o = (o.reshape(b, s, B_VW) * jax.nn.silu(gate)).astype(hn.dtype)
    return o @ w_out


def setup_inputs(seed: int = 0) -> dict:
    key = jax.random.key(seed)
    ks = jax.random.split(key, 16)
    f32 = jnp.float32
    nrm = lambda k, shape, scale: jax.random.normal(k, shape, f32) * scale
    dt = jnp.exp(jax.random.uniform(ks[10], (2, B_HEADS), f32, np.log(1e-3), np.log(1e-1)))
    return {
        "x": jax.random.normal(ks[0], (BATCH, SEQ, D_MODEL), f32),
        "attn_norm": 1.0 + nrm(ks[1], (D_MODEL,), 0.02),
        "attn_w_in": nrm(ks[2], (D_MODEL, A_IN), D_MODEL ** -0.5),
        "attn_sink": nrm(ks[3], (A_HEADS,), 0.5),
        "attn_w_out": nrm(ks[4], (A_WIDTH, D_MODEL), A_WIDTH ** -0.5),
        "delta_norm": 1.0 + nrm(ks[5], (D_MODEL,), 0.02),
        "delta_w_in": nrm(ks[6], (D_MODEL, B_IN), D_MODEL ** -0.5),
        "delta_conv": nrm(ks[7], (CONV_WIDTH, 2 * B_KW + B_VW), CONV_WIDTH ** -0.5),
        "delta_a_log": jnp.log(jax.random.uniform(ks[8], (2, B_HEADS), f32, 1.0, 16.0)),
        "delta_dt_bias": dt + jnp.log(-jnp.expm1(-dt)),
        "delta_out_norm": 1.0 + nrm(ks[9], (B_HEAD_V,), 0.02),
        "delta_w_out": nrm(ks[11], (B_VW, D_MODEL), B_VW ** -0.5),
        "final_norm": 1.0 + nrm(ks[12], (D_MODEL,), 0.02),
    }


def reference(x, attn_norm, attn_w_in, attn_sink, attn_w_out,
              delta_norm, delta_w_in, delta_conv, delta_a_log, delta_dt_bias,
              delta_out_norm, delta_w_out, final_norm):
    layer_params = (
        (attn_norm, attn_w_in, attn_sink, attn_w_out),
        (delta_norm, delta_w_in, delta_conv, delta_a_log, delta_dt_bias, delta_out_norm, delta_w_out),
    )
    for i in range(DEPTH):
        p = layer_params[i]
        hn = rms_norm(x, p[0])
        if i % N_MIXERS == 0:
            x = x + attention_mixer(hn, *p[1:])
        else:
            x = x + deltanet_mixer(hn, *p[1:])
    return rms_norm(x, final_norm)
```

```python
import functools

import numpy as np
import jax
import jax.numpy as jnp
from jax import lax
from jax.experimental import pallas as pl
from jax.experimental.pallas import tpu as pltpu

F32 = jnp.float32
BF16 = jnp.bfloat16
EPS = 1e-6

D_MODEL = 1024
A_HEADS = 16
A_KV_HEADS = 4
A_HEAD_DIM = 64
A_PAIRS = A_HEADS // 2
BLOCK = 128
KEYS = 3 * BLOCK

B_HEADS = 8
B_HEAD_K = 128
B_HEAD_V = 128
CONV_WIDTH = 5
CONV_PAD = CONV_WIDTH // 2
CHUNK = 64
N_QKV = 3 * B_HEADS * B_HEAD_K
HALO = 8

V7X_VMEM_LIMIT_BYTES = 56 * 1024 * 1024


def _sigmoid(x):
    return 1.0 / (1.0 + jnp.exp(-x))


def _softplus(x):
    return jnp.maximum(x, 0.0) + jnp.log(1.0 + jnp.exp(-jnp.abs(x)))


def _rms_rows(x, gain):
    ms = jnp.mean(x * x, axis=-1, keepdims=True)
    return x * lax.rsqrt(ms + EPS) * gain


def _split3(x):
    a = x.astype(BF16)
    r = x - a.astype(F32)
    b = r.astype(BF16)
    c = (r - b.astype(F32)).astype(BF16)
    return a, b, c


def _dot(a, b):
    return jnp.dot(a, b, preferred_element_type=F32)


def _dot_nt(a, b):
    return lax.dot_general(a, b, (((1,), (1,)), ((), ())), preferred_element_type=F32)


def _dot_tn(a, b):
    return lax.dot_general(a, b, (((0,), (0,)), ((), ())), preferred_element_type=F32)


def _params(semantics):
    return pltpu.CompilerParams(dimension_semantics=semantics, vmem_limit_bytes=V7X_VMEM_LIMIT_BYTES)


def _attn_proj_kernel(x_ref, gain_ref, w_ref, q_ref, kv_ref, gate_ref):
    hn = _rms_rows(x_ref[...], gain_ref[...]).astype(BF16)
    proj = _dot(hn, w_ref[...])
    q_ref[...] = (proj[:, :D_MODEL] * (A_HEAD_DIM ** -0.5)).astype(BF16)
    kv_ref[...] = proj[:, D_MODEL:2 * D_MODEL].astype(BF16)
    gate_ref[...] = proj[:, 2 * D_MODEL:].astype(BF16)


def _attn_proj(x2, gain, w):
    t = x2.shape[0]
    tm = 512
    assert t % tm == 0
    out = jax.ShapeDtypeStruct((t, D_MODEL), BF16)
    row = pl.BlockSpec((tm, D_MODEL), lambda i: (i, 0))
    return pl.pallas_call(
        _attn_proj_kernel,
        grid=(t // tm,),
        in_specs=[row, pl.BlockSpec((1, D_MODEL), lambda i: (0, 0)), pl.BlockSpec(w.shape, lambda i: (0, 0))],
        out_specs=[row, row, row],
        out_shape=[out, out, out],
        compiler_params=_params(("parallel",)),
        name="attn_proj",
    )(x2, gain, w)


def _attn_core_kernel(sink_ref, q_ref, kvp_ref, kvc_ref, kvn_ref, gate_ref, x_ref, bias_ref, edge_ref, w_ref,
                      o_ref, kv_scr, og_scr, *, tq, n_tiles):
    j = pl.program_id(1)
    kv_scr[0:BLOCK, :] = kvp_ref[...]
    kv_scr[BLOCK:BLOCK + tq, :] = kvc_ref[...]
    kv_scr[BLOCK + tq:, :] = kvn_ref[...]
    lane = lax.broadcasted_iota(jnp.int32, (KEYS, 2 * A_HEAD_DIM), 1)
    low = lane < A_HEAD_DIM
    low_q = lax.broadcasted_iota(jnp.int32, (BLOCK, 2 * A_HEAD_DIM), 1) < A_HEAD_DIM
    zero = jnp.zeros((KEYS, 2 * A_HEAD_DIM), BF16)
    n_qb = tq // BLOCK

    def block(qb, carry):
        r0 = pl.multiple_of(qb * BLOCK, BLOCK)
        gqb = j * n_qb + qb
        case = jnp.where(gqb == 0, 1, jnp.where(gqb == n_tiles * n_qb - 1, 2, 0))
        edge = edge_ref[case]
        for g in range(A_KV_HEADS):
            k_rep = kv_scr[pl.ds(r0, KEYS), g * 128:(g + 1) * 128]
            v_rep = kv_scr[pl.ds(r0, KEYS), 512 + g * 128:512 + (g + 1) * 128]
            k_bd = jnp.concatenate([jnp.where(low, k_rep, zero), jnp.where(low, zero, k_rep)], axis=0)
            v_bd = jnp.concatenate([jnp.where(low, v_rep, zero), jnp.where(low, zero, v_rep)], axis=0)
            for pr in range(2):
                pair = 2 * g + pr
                q_pair = q_ref[pl.ds(r0, BLOCK), pair * 128:(pair + 1) * 128]
                s = _dot_nt(q_pair, k_bd) + bias_ref[pair] + edge
                probs, inv = [], []
                for hh in range(2):
                    sink = sink_ref[2 * pair + hh]
                    sh = s[:, hh * KEYS:(hh + 1) * KEYS]
                    m = jnp.maximum(jnp.max(sh, axis=-1, keepdims=True), sink)
                    p = jnp.exp(sh - m)
                    denom = jnp.sum(p, axis=-1, keepdims=True) + jnp.exp(sink - m)
                    probs.append(p.astype(BF16))
                    inv.append(1.0 / denom)
                o_pair = _dot(jnp.concatenate(probs, axis=1), v_bd)
                o_pair = o_pair * jnp.where(low_q, inv[0], inv[1])
                gate = gate_ref[pl.ds(r0, BLOCK), pair * 128:(pair + 1) * 128].astype(F32)
                og_scr[pl.ds(r0, BLOCK), pair * 128:(pair + 1) * 128] = (o_pair * (gate * _sigmoid(gate))).astype(BF16)
        return carry

    lax.fori_loop(0, n_qb, block, 0)
    o_ref[...] = x_ref[...] + _dot(og_scr[...], w_ref[...])


def _attn_tables(seq):
    qpos = np.arange(BLOCK)[:, None]
    krel = np.arange(KEYS)[None, :] - BLOCK
    dist = np.abs(krel - qpos).astype(np.float32)
    slopes = np.power(2.0, -8.0 * np.arange(1, A_HEADS + 1) / A_HEADS).astype(np.float32)
    band = np.where(dist <= BLOCK, 0.0, -np.inf).astype(np.float32)
    per_head = -slopes[:, None, None] * dist[None] + band[None]
    bias = per_head.reshape(A_PAIRS, 2, BLOCK, KEYS).transpose(0, 2, 1, 3).reshape(A_PAIRS, BLOCK, 2 * KEYS)
    edge = np.zeros((3, BLOCK, 2, KEYS), np.float32)
    edge[1, :, :, :BLOCK] = -np.inf
    edge[2, :, :, 2 * BLOCK:] = -np.inf
    return jnp.asarray(bias), jnp.asarray(edge.reshape(3, BLOCK, 2 * KEYS))


def _attn_core(x, q, kv, gate, sink, w_out):
    b, s, _ = x.shape
    tq = 512
    assert s % tq == 0 and s // BLOCK >= 2
    n_tiles = s // tq
    nb = s // BLOCK
    per = tq // BLOCK
    bias, edge = _attn_tables(s)
    tile = pl.BlockSpec((None, tq, D_MODEL), lambda i, j: (i, j, 0))
    kernel = functools.partial(_attn_core_kernel, tq=tq, n_tiles=n_tiles)
    return pl.pallas_call(
        kernel,
        grid=(b, n_tiles),
        in_specs=[
            pl.BlockSpec(memory_space=pltpu.SMEM),
            tile,
            pl.BlockSpec((None, BLOCK, D_MODEL), lambda i, j: (i, jnp.maximum(j * per - 1, 0), 0)),
            tile,
            pl.BlockSpec((None, BLOCK, D_MODEL), lambda i, j: (i, jnp.minimum((j + 1) * per, nb - 1), 0)),
            tile,
            tile,
            pl.BlockSpec(bias.shape, lambda i, j: (0, 0, 0)),
            pl.BlockSpec(edge.shape, lambda i, j: (0, 0, 0)),
            pl.BlockSpec(w_out.shape, lambda i, j: (0, 0)),
        ],
        out_specs=tile,
        out_shape=jax.ShapeDtypeStruct(x.shape, F32),
        scratch_shapes=[pltpu.VMEM((tq + 2 * BLOCK, D_MODEL), BF16), pltpu.VMEM((tq, D_MODEL), BF16)],
        compiler_params=_params(("parallel", "parallel")),
        name="attn_core",
    )(sink, q, kv, kv, kv, gate, x, bias, edge, w_out)


def _delta_proj_kernel(xp_ref, xc_ref, xn_ref, gain_ref, wqkv_ref, wg_ref, wba_ref, wbat_ref, conv_ref,
                       alog_ref, dtb_ref, alogc_ref, dtbc_ref, tril_ref, triu_ref, trilt_ref, triut_ref,
                       q_ref, k_ref, v_ref, gate_ref, col_ref, row_ref, hn_scr, proj_scr, *, tm, n_tiles):
    j = pl.program_id(1)
    gain = gain_ref[...]
    keep_prev = (j > 0).astype(F32)
    keep_next = (j < n_tiles - 1).astype(F32)
    hn_scr[0:HALO, :] = (_rms_rows(xp_ref[...], gain) * keep_prev).astype(BF16)
    hn_scr[HALO:HALO + tm, :] = _rms_rows(xc_ref[...], gain).astype(BF16)
    hn_scr[HALO + tm:, :] = (_rms_rows(xn_ref[...], gain) * keep_next).astype(BF16)
    proj_scr[...] = _dot(hn_scr[...], wqkv_ref[...])
    hn = hn_scr[HALO:HALO + tm, :]
    gate_ref[...] = _dot(hn, wg_ref[...]).astype(BF16)

    ba = _dot(hn, wba_ref[...])
    g_c = -jnp.exp(alog_ref[...]) * _softplus(ba + dtb_ref[...])
    bat = _dot_nt(wbat_ref[...], hn)
    g_r = -jnp.exp(alogc_ref[...]) * _softplus(bat + dtbc_ref[...])
    gc3 = _split3(g_c)
    gr3 = _split3(g_r)
    ent_r = lax.broadcasted_iota(jnp.int32, (4 * B_HEADS, tm), 0)
    cum_r = jnp.where(ent_r < 3 * B_HEADS, sum(_dot(p, trilt_ref[...]) for p in gr3),
                      sum(_dot(p, triut_ref[...]) for p in gr3))
    row_ref[...] = jnp.where(ent_r < 2 * B_HEADS, _sigmoid(bat), cum_r)
    ent_c = lax.broadcasted_iota(jnp.int32, (CHUNK, 4 * B_HEADS), 1)
    beta_c = _sigmoid(ba)
    for c in range(tm // CHUNK):
        rows = slice(c * CHUNK, (c + 1) * CHUNK)
        fwd = sum(_dot(tril_ref[...], p[rows]) for p in gc3)
        bwd = sum(_dot(triu_ref[...], p[rows]) for p in gc3)
        col_ref[rows, :] = jnp.where(ent_c < 2 * B_HEADS, beta_c[rows], jnp.where(ent_c < 3 * B_HEADS, fwd, bwd))

    taps = conv_ref[...]
    rc = 128

    def rows_block(r, carry):
        r0 = pl.multiple_of(r * rc, rc)
        for cb in range(N_QKV // 128):
            cols = slice(cb * 128, (cb + 1) * 128)
            window = proj_scr[pl.ds(r0, rc + 2 * HALO), cols]
            acc = None
            for t in range(CONV_WIDTH):
                lo = HALO - CONV_PAD + t
                term = taps[t:t + 1, cols] * window[lo:lo + rc]
                acc = term if acc is None else acc + term
            y = acc * _sigmoid(acc)
            if cb < 2 * B_HEADS:
                y = y * lax.rsqrt(jnp.sum(y * y, axis=-1, keepdims=True) + EPS)
            if cb < B_HEADS:
                q_ref[pl.ds(r0, rc), cols] = (y * (B_HEAD_K ** -0.5)).astype(BF16)
            elif cb < 2 * B_HEADS:
                k_ref[pl.ds(r0, rc), slice((cb - B_HEADS) * 128, (cb - B_HEADS + 1) * 128)] = y.astype(BF16)
            else:
                v_ref[pl.ds(r0, rc), slice((cb - 2 * B_HEADS) * 128, (cb - 2 * B_HEADS + 1) * 128)] = y.astype(BF16)
        return carry

    lax.fori_loop(0, tm // rc, rows_block, 0)


def _delta_proj(x1, gain, w_in, conv_w, a_log, dt_bias):
    b, s, _ = x1.shape
    tm = 512
    assert s % tm == 0
    n_tiles = s // tm
    per = tm // HALO
    nh = s // HALO
    kw = B_HEADS * B_HEAD_K
    wqkv = w_in[:, :N_QKV].astype(BF16)
    wg = w_in[:, N_QKV:N_QKV + kw].astype(BF16)
    wba = w_in[:, N_QKV + kw:].astype(BF16)
    wbat = wba.T
    pad = jnp.zeros((1, 2 * B_HEADS), F32)
    alog = jnp.concatenate([pad, a_log.reshape(1, 2 * B_HEADS).astype(F32)], axis=1)
    dtb = jnp.concatenate([pad, dt_bias.reshape(1, 2 * B_HEADS).astype(F32)], axis=1)
    i = np.arange(CHUNK)
    tril = (i[:, None] >= i[None, :]).astype(np.float32)
    triu = tril.T
    eye = np.eye(tm // CHUNK, dtype=np.float32)
    trilt = jnp.asarray(np.kron(eye, tril.T), BF16)
    triut = jnp.asarray(np.kron(eye, triu.T), BF16)
    tril = jnp.asarray(tril, BF16)
    triu = jnp.asarray(triu, BF16)
    full = lambda a: pl.BlockSpec(a.shape, lambda i_, j_: (0,) * a.ndim)
    tile = pl.BlockSpec((None, tm, D_MODEL), lambda i_, j_: (i_, j_, 0))
    act = jax.ShapeDtypeStruct((b, s, kw), BF16)
    kernel = functools.partial(_delta_proj_kernel, tm=tm, n_tiles=n_tiles)
    consts = [gain, wqkv, wg, wba, wbat, conv_w.astype(F32), alog, dtb, alog.T, dtb.T, tril, triu, trilt, triut]
    return pl.pallas_call(
        kernel,
        grid=(b, n_tiles),
        in_specs=[
            pl.BlockSpec((None, HALO, D_MODEL), lambda i_, j_: (i_, jnp.maximum(j_ * per - 1, 0), 0)),
            tile,
            pl.BlockSpec((None, HALO, D_MODEL), lambda i_, j_: (i_, jnp.minimum((j_ + 1) * per, nh - 1), 0)),
        ] + [full(a) for a in consts],
        out_specs=[tile, tile, tile, tile,
                   pl.BlockSpec((None, tm, 4 * B_HEADS), lambda i_, j_: (i_, j_, 0)),
                   pl.BlockSpec((None, 4 * B_HEADS, tm), lambda i_, j_: (i_, 0, j_))],
        out_shape=[act, act, act, act,
                   jax.ShapeDtypeStruct((b, s, 4 * B_HEADS), F32),
                   jax.ShapeDtypeStruct((b, 4 * B_HEADS, s), F32)],
        scratch_shapes=[pltpu.VMEM((tm + 2 * HALO, D_MODEL), BF16), pltpu.VMEM((tm + 2 * HALO, N_QKV), F32)],
        compiler_params=_params(("parallel", "parallel")),
        name="delta_proj",
    )(x1, x1, x1, *consts)


def _unit_triangular_inverse(low):
    n = low.shape[0]
    eye = (lax.broadcasted_iota(jnp.int32, (n, n), 0) == lax.broadcasted_iota(jnp.int32, (n, n), 1)).astype(F32)
    x = -low
    p = eye + x
    power = 1
    while 2 * power < n:
        xb = x.astype(BF16)
        x = _dot(xb, xb)
        p = p + _dot(p.astype(BF16), x.astype(BF16))
        power *= 2
    return p


def _delta_scan_kernel(qf_ref, kf_ref, vf_ref, colf_ref, rowf_ref, qb_ref, kb_ref, vb_ref, colb_ref, rowb_ref,
                       of_ref, ob_ref, state_ref):
    @pl.when(pl.program_id(1) == 0)
    def _():
        state_ref[...] = jnp.zeros_like(state_ref)

    ri = lax.broadcasted_iota(jnp.int32, (CHUNK, CHUNK), 0)
    ci = lax.broadcasted_iota(jnp.int32, (CHUNK, CHUNK), 1)
    dirs = (
        (qf_ref, kf_ref, vf_ref, colf_ref, rowf_ref, of_ref, ri >= ci, ri > ci, CHUNK - 1),
        (qb_ref, kb_ref, vb_ref, colb_ref, rowb_ref, ob_ref, ri <= ci, ri < ci, 0),
    )
    for d, (q_ref, k_ref, v_ref, col_ref, row_ref, o_ref, incl, strict, last) in enumerate(dirs):
        for h in range(B_HEADS):
            cols = slice(h * 128, (h + 1) * 128)
            idx = d * B_HEADS + h
            qh, kh, vh = q_ref[:, cols], k_ref[:, cols], v_ref[:, cols]
            beta_c = col_ref[:, idx:idx + 1]
            gc_c = col_ref[:, 2 * B_HEADS + idx:2 * B_HEADS + idx + 1]
            gc_r = row_ref[2 * B_HEADS + idx:2 * B_HEADS + idx + 1, :]
            g_last = gc_r[:, last:last + 1]
            gram = _dot_nt(jnp.concatenate([qh, kh], axis=0), kh)
            decay = jnp.where(incl, jnp.exp(jnp.where(incl, gc_c - gc_r, 0.0)), 0.0)
            qk = gram[:CHUNK] * decay
            low = jnp.where(strict, gram[CHUNK:] * decay * beta_c, 0.0)
            t_inv = _unit_triangular_inverse(low)
            kf32 = kh.astype(F32)
            e_c = jnp.exp(gc_c)
            kbg = (kf32 * (beta_c * e_c)).astype(BF16)
            qg = (qh.astype(F32) * e_c).astype(BF16)
            kd = (kf32 * jnp.exp(g_last - gc_c)).astype(BF16)
            vbeta = vh.astype(F32) * beta_c
            state = state_ref[idx]
            rs = _dot(jnp.concatenate([kbg, qg], axis=0), state.astype(BF16))
            v_new = _dot(t_inv.astype(BF16), (vbeta - rs[:CHUNK]).astype(BF16)).astype(BF16)
            o_ref[:, cols] = (rs[CHUNK:] + _dot(qk.astype(BF16), v_new)).astype(o_ref.dtype)
            state_ref[idx] = state * jnp.exp(g_last) + _dot_tn(kd, v_new)


def _delta_scan(q, k, v, col, row):
    b, s, kw = q.shape
    nc = s // CHUNK
    row4 = row.reshape(b, 4 * B_HEADS, nc, CHUNK).transpose(0, 2, 1, 3)
    fwd = lambda i, c: (i, c, 0)
    bwd = lambda i, c: (i, nc - 1 - c, 0)
    act = lambda m: pl.BlockSpec((None, CHUNK, kw), m)
    colspec = lambda m: pl.BlockSpec((None, CHUNK, 4 * B_HEADS), m)
    rowspec = lambda m: pl.BlockSpec((None, None, 4 * B_HEADS, CHUNK), lambda i, c: m(i, c) + (0,))
    out = jax.ShapeDtypeStruct((b, s, kw), BF16)
    return pl.pallas_call(
        _delta_scan_kernel,
        grid=(b, nc),
        in_specs=[act(fwd), act(fwd), act(fwd), colspec(fwd), rowspec(fwd),
                  act(bwd), act(bwd), act(bwd), colspec(bwd), rowspec(bwd)],
        out_specs=[act(fwd), act(bwd)],
        out_shape=[out, out],
        scratch_shapes=[pltpu.VMEM((2 * B_HEADS, B_HEAD_K, B_HEAD_V), F32)],
        compiler_params=_params(("parallel", "arbitrary")),
        name="delta_scan",
    )(q, k, v, col, row4, q, k, v, col, row4)


def _delta_out_kernel(of_ref, ob_ref, gate_ref, x_ref, onorm_ref, w_ref, fnorm_ref, o_ref, og_scr):
    onorm = onorm_ref[...]
    for h in range(B_HEADS):
        cols = slice(h * 128, (h + 1) * 128)
        o = of_ref[:, cols].astype(F32) + ob_ref[:, cols].astype(F32)
        o = o * lax.rsqrt(jnp.mean(o * o, axis=-1, keepdims=True) + EPS) * onorm
        gate = gate_ref[:, cols].astype(F32)
        og_scr[:, cols] = (o * (gate * _sigmoid(gate))).astype(BF16)
    y = x_ref[...] + _dot(og_scr[...], w_ref[...])
    o_ref[...] = _rms_rows(y, fnorm_ref[...])


def _delta_out(o_f, o_b, gate, x1, out_norm, w_out, final_norm):
    t = x1.shape[0]
    tm = 512
    assert t % tm == 0
    row = pl.BlockSpec((tm, D_MODEL), lambda i: (i, 0))
    vec = lambda n: pl.BlockSpec((1, n), lambda i: (0, 0))
    return pl.pallas_call(
        _delta_out_kernel,
        grid=(t // tm,),
        in_specs=[row, row, row, row, vec(B_HEAD_V), pl.BlockSpec(w_out.shape, lambda i: (0, 0)), vec(D_MODEL)],
        out_specs=row,
        out_shape=jax.ShapeDtypeStruct((t, D_MODEL), F32),
        scratch_shapes=[pltpu.VMEM((tm, D_MODEL), BF16)],
        compiler_params=_params(("parallel",)),
        name="delta_out",
    )(o_f, o_b, gate, x1, out_norm, w_out, final_norm)


def _attn_weights(w_in):
    kvw = A_KV_HEADS * A_HEAD_DIM
    wq = w_in[:, :D_MODEL]
    wk = w_in[:, D_MODEL:D_MODEL + kvw].reshape(D_MODEL, A_KV_HEADS, 1, A_HEAD_DIM)
    wv = w_in[:, D_MODEL + kvw:D_MODEL + 2 * kvw].reshape(D_MODEL, A_KV_HEADS, 1, A_HEAD_DIM)
    rep = lambda w: jnp.broadcast_to(w, (D_MODEL, A_KV_HEADS, 2, A_HEAD_DIM)).reshape(D_MODEL, 2 * kvw)
    wg = w_in[:, D_MODEL + 2 * kvw:]
    return jnp.concatenate([wq, rep(wk), rep(wv), wg], axis=1).astype(BF16)


def kernel(x, attn_norm, attn_w_in, attn_sink, attn_w_out, delta_norm, delta_w_in, delta_conv, delta_a_log,
           delta_dt_bias, delta_out_norm, delta_w_out, final_norm):
    b, s, d = x.shape
    assert d == D_MODEL
    t = b * s
    q, kv, gate = _attn_proj(x.reshape(t, d), attn_norm.reshape(1, d).astype(F32), _attn_weights(attn_w_in))
    shape3 = (b, s, d)
    x1 = _attn_core(x, q.reshape(shape3), kv.reshape(shape3), gate.reshape(shape3), attn_sink.astype(F32),
                    attn_w_out.astype(BF16))
    dq, dk, dv, dgate, col, row = _delta_proj(x1, delta_norm.reshape(1, d).astype(F32), delta_w_in, delta_conv,
                                              delta_a_log, delta_dt_bias)
    o_f, o_b = _delta_scan(dq, dk, dv, col, row)
    out = _delta_out(o_f.reshape(t, d), o_b.reshape(t, d), dgate.reshape(t, d), x1.reshape(t, d),
                     delta_out_norm.reshape(1, B_HEAD_V).astype(F32), delta_w_out.astype(BF16),
                     final_norm.reshape(1, d).astype(F32))
    return out.reshape(b, s, d)
```

```python
import functools

import numpy as np
import jax
import jax.numpy as jnp
from jax import lax
from jax.experimental import pallas as pl
from jax.experimental.pallas import tpu as pltpu

F32 = jnp.float32
BF16 = jnp.bfloat16
EPS = 1e-6

D_MODEL = 1024
A_HEADS = 16
A_KV_HEADS = 4
A_HEAD_DIM = 64
A_PAIRS = A_HEADS // 2
BLOCK = 128
KEYS = 3 * BLOCK

B_HEADS = 8
B_HEAD_K = 128
B_HEAD_V = 128
CONV_WIDTH = 5
CONV_PAD = CONV_WIDTH // 2
CHUNK = 64
N_QKV = 3 * B_HEADS * B_HEAD_K
HALO = 8

V7X_VMEM_LIMIT_BYTES = 56 * 1024 * 1024


def _sigmoid(x):
    return 1.0 / (1.0 + jnp.exp(-x))


def _softplus(x):
    return jnp.maximum(x, 0.0) + jnp.log(1.0 + jnp.exp(-jnp.abs(x)))


def _rms_rows(x, gain):
    ms = jnp.mean(x * x, axis=-1, keepdims=True)
    return x * lax.rsqrt(ms + EPS) * gain


def _split3(x):
    a = x.astype(BF16)
    r = x - a.astype(F32)
    b = r.astype(BF16)
    c = (r - b.astype(F32)).astype(BF16)
    return a, b, c


def _dot(a, b):
    return jnp.dot(a, b, preferred_element_type=F32)


def _dot_nt(a, b):
    return lax.dot_general(a, b, (((1,), (1,)), ((), ())), preferred_element_type=F32)


def _dot_tn(a, b):
    return lax.dot_general(a, b, (((0,), (0,)), ((), ())), preferred_element_type=F32)


def _params(semantics):
    return pltpu.CompilerParams(dimension_semantics=semantics, vmem_limit_bytes=V7X_VMEM_LIMIT_BYTES)


def _attn_proj_kernel(x_ref, gain_ref, w_ref, q_ref, kv_ref, gate_ref):
    hn = _rms_rows(x_ref[...], gain_ref[...]).astype(BF16)
    proj = _dot(hn, w_ref[...])
    q_ref[...] = (proj[:, :D_MODEL] * (A_HEAD_DIM ** -0.5)).astype(BF16)
    kv_ref[...] = proj[:, D_MODEL:2 * D_MODEL].astype(BF16)
    gate_ref[...] = proj[:, 2 * D_MODEL:].astype(BF16)


def _attn_proj(x2, gain, w):
    t = x2.shape[0]
    tm = 512
    assert t % tm == 0
    out = jax.ShapeDtypeStruct((t, D_MODEL), BF16)
    row = pl.BlockSpec((tm, D_MODEL), lambda i: (i, 0))
    return pl.pallas_call(
        _attn_proj_kernel,
        grid=(t // tm,),
        in_specs=[row, pl.BlockSpec((1, D_MODEL), lambda i: (0, 0)), pl.BlockSpec(w.shape, lambda i: (0, 0))],
        out_specs=[row, row, row],
        out_shape=[out, out, out],
        compiler_params=_params(("parallel",)),
        name="attn_proj",
    )(x2, gain, w)


def _attn_core_kernel(sink_ref, q_ref, kvp_ref, kvc_ref, kvn_ref, gate_ref, x_ref, bias_ref, edge_ref, w_ref,
                      o_ref, kv_scr, og_scr, *, tq, n_tiles):
    j = pl.program_id(1)
    kv_scr[0:BLOCK, :] = kvp_ref[...]
    kv_scr[BLOCK:BLOCK + tq, :] = kvc_ref[...]
    kv_scr[BLOCK + tq:, :] = kvn_ref[...]
    lane = lax.broadcasted_iota(jnp.int32, (KEYS, 2 * A_HEAD_DIM), 1)
    low = lane < A_HEAD_DIM
    low_q = lax.broadcasted_iota(jnp.int32, (BLOCK, 2 * A_HEAD_DIM), 1) < A_HEAD_DIM
    zero = jnp.zeros((KEYS, 2 * A_HEAD_DIM), BF16)
    n_qb = tq // BLOCK

    def block(qb, carry):
        r0 = pl.multiple_of(qb * BLOCK, BLOCK)
        gqb = j * n_qb + qb
        case = jnp.where(gqb == 0, 1, jnp.where(gqb == n_tiles * n_qb - 1, 2, 0))
        edge = edge_ref[case]
        for g in range(A_KV_HEADS):
            k_rep = kv_scr[pl.ds(r0, KEYS), g * 128:(g + 1) * 128]
            v_rep = kv_scr[pl.ds(r0, KEYS), 512 + g * 128:512 + (g + 1) * 128]
            k_bd = jnp.concatenate([jnp.where(low, k_rep, zero), jnp.where(low, zero, k_rep)], axis=0)
            v_bd = jnp.concatenate([jnp.where(low, v_rep, zero), jnp.where(low, zero, v_rep)], axis=0)
            for pr in range(2):
                pair = 2 * g + pr
                q_pair = q_ref[pl.ds(r0, BLOCK), pair * 128:(pair + 1) * 128]
                s = _dot_nt(q_pair, k_bd) + bias_ref[pair] + edge
                probs, inv = [], []
                for hh in range(2):
                    sink = sink_ref[2 * pair + hh]
                    sh = s[:, hh * KEYS:(hh + 1) * KEYS]
                    m = jnp.maximum(jnp.max(sh, axis=-1, keepdims=True), sink)
                    p = jnp.exp(sh - m)
                    denom = jnp.sum(p, axis=-1, keepdims=True) + jnp.exp(sink - m)
                    probs.append(p.astype(BF16))
                    inv.append(1.0 / denom)
                o_pair = _dot(jnp.concatenate(probs, axis=1), v_bd)
                o_pair = o_pair * jnp.where(low_q, inv[0], inv[1])
                gate = gate_ref[pl.ds(r0, BLOCK), pair * 128:(pair + 1) * 128].astype(F32)
                og_scr[pl.ds(r0, BLOCK), pair * 128:(pair + 1) * 128] = (o_pair * (gate * _sigmoid(gate))).astype(BF16)
        return carry

    lax.fori_loop(0, n_qb, block, 0)
    o_ref[...] = x_ref[...] + _dot(og_scr[...], w_ref[...])


def _attn_tables(seq):
    qpos = np.arange(BLOCK)[:, None]
    krel = np.arange(KEYS)[None, :] - BLOCK
    dist = np.abs(krel - qpos).astype(np.float32)
    slopes = np.power(2.0, -8.0 * np.arange(1, A_HEADS + 1) / A_HEADS).astype(np.float32)
    band = np.where(dist <= BLOCK, 0.0, -np.inf).astype(np.float32)
    per_head = -slopes[:, None, None] * dist[None] + band[None]
    bias = per_head.reshape(A_PAIRS, 2, BLOCK, KEYS).transpose(0, 2, 1, 3).reshape(A_PAIRS, BLOCK, 2 * KEYS)
    edge = np.zeros((3, BLOCK, 2, KEYS), np.float32)
    edge[1, :, :, :BLOCK] = -np.inf
    edge[2, :, :, 2 * BLOCK:] = -np.inf
    return jnp.asarray(bias), jnp.asarray(edge.reshape(3, BLOCK, 2 * KEYS))


def _attn_core(x, q, kv, gate, sink, w_out):
    b, s, _ = x.shape
    tq = 512
    assert s % tq == 0 and s // BLOCK >= 2
    n_tiles = s // tq
    nb = s // BLOCK
    per = tq // BLOCK
    bias, edge = _attn_tables(s)
    tile = pl.BlockSpec((None, tq, D_MODEL), lambda i, j: (i, j, 0))
    kernel = functools.partial(_attn_core_kernel, tq=tq, n_tiles=n_tiles)
    return pl.pallas_call(
        kernel,
        grid=(b, n_tiles),
        in_specs=[
            pl.BlockSpec(memory_space=pltpu.SMEM),
            tile,
            pl.BlockSpec((None, BLOCK, D_MODEL), lambda i, j: (i, jnp.maximum(j * per - 1, 0), 0)),
            tile,
            pl.BlockSpec((None, BLOCK, D_MODEL), lambda i, j: (i, jnp.minimum((j + 1) * per, nb - 1), 0)),
            tile,
            tile,
            pl.BlockSpec(bias.shape, lambda i, j: (0, 0, 0)),
            pl.BlockSpec(edge.shape, lambda i, j: (0, 0, 0)),
            pl.BlockSpec(w_out.shape, lambda i, j: (0, 0)),
        ],
        out_specs=tile,
        out_shape=jax.ShapeDtypeStruct(x.shape, F32),
        scratch_shapes=[pltpu.VMEM((tq + 2 * BLOCK, D_MODEL), BF16), pltpu.VMEM((tq, D_MODEL), BF16)],
        compiler_params=_params(("parallel", "parallel")),
        name="attn_core",
    )(sink, q, kv, kv, kv, gate, x, bias, edge, w_out)


def _delta_proj_kernel(xp_ref, xc_ref, xn_ref, gain_ref, wqkv_ref, wg_ref, wba_ref, wbat_ref, conv_ref,
                       alog_ref, dtb_ref, alogc_ref, dtbc_ref, tril_ref, triu_ref, trilt_ref, triut_ref,
                       q_ref, k_ref, v_ref, gate_ref, col_ref, row_ref, hn_scr, proj_scr, *, tm, n_tiles):
    j = pl.program_id(1)
    gain = gain_ref[...]
    keep_prev = (j > 0).astype(F32)
    keep_next = (j < n_tiles - 1).astype(F32)
    hn_scr[0:HALO, :] = (_rms_rows(xp_ref[...], gain) * keep_prev).astype(BF16)
    hn_scr[HALO:HALO + tm, :] = _rms_rows(xc_ref[...], gain).astype(BF16)
    hn_scr[HALO + tm:, :] = (_rms_rows(xn_ref[...], gain) * keep_next).astype(BF16)
    proj_scr[...] = _dot(hn_scr[...], wqkv_ref[...])
    hn = hn_scr[HALO:HALO + tm, :]
    gate_ref[...] = _dot(hn, wg_ref[...]).astype(BF16)

    ba = _dot(hn, wba_ref[...])
    g_c = -jnp.exp(alog_ref[...]) * _softplus(ba + dtb_ref[...])
    bat = _dot_nt(wbat_ref[...], hn)
    g_r = -jnp.exp(alogc_ref[...]) * _softplus(bat + dtbc_ref[...])
    gc3 = _split3(g_c)
    gr3 = _split3(g_r)
    ent_r = lax.broadcasted_iota(jnp.int32, (4 * B_HEADS, tm), 0)
    cum_r = jnp.where(ent_r < 3 * B_HEADS, sum(_dot(p, trilt_ref[...]) for p in gr3),
                      sum(_dot(p, triut_ref[...]) for p in gr3))
    row_ref[...] = jnp.where(ent_r < 2 * B_HEADS, _sigmoid(bat), cum_r)
    ent_c = lax.broadcasted_iota(jnp.int32, (CHUNK, 4 * B_HEADS), 1)
    beta_c = _sigmoid(ba)
    for c in range(tm // CHUNK):
        rows = slice(c * CHUNK, (c + 1) * CHUNK)
        fwd = sum(_dot(tril_ref[...], p[rows]) for p in gc3)
        bwd = sum(_dot(triu_ref[...], p[rows]) for p in gc3)
        col_ref[rows, :] = jnp.where(ent_c < 2 * B_HEADS, beta_c[rows], jnp.where(ent_c < 3 * B_HEADS, fwd, bwd))

    taps = conv_ref[...]
    rc = 128

    def rows_block(r, carry):
        r0 = pl.multiple_of(r * rc, rc)
        for cb in range(N_QKV // 128):
            cols = slice(cb * 128, (cb + 1) * 128)
            window = proj_scr[pl.ds(r0, rc + 2 * HALO), cols]
            acc = None
            for t in range(CONV_WIDTH):
                lo = HALO - CONV_PAD + t
                term = taps[t:t + 1, cols] * window[lo:lo + rc]
                acc = term if acc is None else acc + term
            y = acc * _sigmoid(acc)
            if cb < 2 * B_HEADS:
                y = y * lax.rsqrt(jnp.sum(y * y, axis=-1, keepdims=True) + EPS)
            if cb < B_HEADS:
                q_ref[pl.ds(r0, rc), cols] = (y * (B_HEAD_K ** -0.5)).astype(BF16)
            elif cb < 2 * B_HEADS:
                k_ref[pl.ds(r0, rc), slice((cb - B_HEADS) * 128, (cb - B_HEADS + 1) * 128)] = y.astype(BF16)
            else:
                v_ref[pl.ds(r0, rc), slice((cb - 2 * B_HEADS) * 128, (cb - 2 * B_HEADS + 1) * 128)] = y.astype(BF16)
        return carry

    lax.fori_loop(0, tm // rc, rows_block, 0)


def _delta_proj(x1, gain, w_in, conv_w, a_log, dt_bias):
    b, s, _ = x1.shape
    tm = 512
    assert s % tm == 0
    n_tiles = s // tm
    per = tm // HALO
    nh = s // HALO
    kw = B_HEADS * B_HEAD_K
    wqkv = w_in[:, :N_QKV].astype(BF16)
    wg = w_in[:, N_QKV:N_QKV + kw].astype(BF16)
    wba = w_in[:, N_QKV + kw:].astype(BF16)
    wbat = wba.T
    pad = jnp.zeros((1, 2 * B_HEADS), F32)
    alog = jnp.concatenate([pad, a_log.reshape(1, 2 * B_HEADS).astype(F32)], axis=1)
    dtb = jnp.concatenate([pad, dt_bias.reshape(1, 2 * B_HEADS).astype(F32)], axis=1)
    i = np.arange(CHUNK)
    tril = (i[:, None] >= i[None, :]).astype(np.float32)
    triu = tril.T
    eye = np.eye(tm // CHUNK, dtype=np.float32)
    trilt = jnp.asarray(np.kron(eye, tril.T), BF16)
    triut = jnp.asarray(np.kron(eye, triu.T), BF16)
    tril = jnp.asarray(tril, BF16)
    triu = jnp.asarray(triu, BF16)
    full = lambda a: pl.BlockSpec(a.shape, lambda i_, j_: (0,) * a.ndim)
    tile = pl.BlockSpec((None, tm, D_MODEL), lambda i_, j_: (i_, j_, 0))
    act = jax.ShapeDtypeStruct((b, s, kw), BF16)
    kernel = functools.partial(_delta_proj_kernel, tm=tm, n_tiles=n_tiles)
    consts = [gain, wqkv, wg, wba, wbat, conv_w.astype(F32), alog, dtb, alog.T, dtb.T, tril, triu, trilt, triut]
    return pl.pallas_call(
        kernel,
        grid=(b, n_tiles),
        in_specs=[
            pl.BlockSpec((None, HALO, D_MODEL), lambda i_, j_: (i_, jnp.maximum(j_ * per - 1, 0), 0)),
            tile,
            pl.BlockSpec((None, HALO, D_MODEL), lambda i_, j_: (i_, jnp.minimum((j_ + 1) * per, nh - 1), 0)),
        ] + [full(a) for a in consts],
        out_specs=[tile, tile, tile, tile,
                   pl.BlockSpec((None, tm, 4 * B_HEADS), lambda i_, j_: (i_, j_, 0)),
                   pl.BlockSpec((None, 4 * B_HEADS, tm), lambda i_, j_: (i_, 0, j_))],
        out_shape=[act, act, act, act,
                   jax.ShapeDtypeStruct((b, s, 4 * B_HEADS), F32),
                   jax.ShapeDtypeStruct((b, 4 * B_HEADS, s), F32)],
        scratch_shapes=[pltpu.VMEM((tm + 2 * HALO, D_MODEL), BF16), pltpu.VMEM((tm + 2 * HALO, N_QKV), F32)],
        compiler_params=_params(("parallel", "parallel")),
        name="delta_proj",
    )(x1, x1, x1, *consts)


PREP_CHUNKS = 4
GROUP = 4
N_GROUPS = B_HEADS // GROUP
PAIR = 2
N_PAIRS = B_HEADS // PAIR


def _block_diag(x, n_blocks):
    r, width = x.shape
    shift = (width // n_blocks).bit_length() - 1
    blk = lax.broadcasted_iota(jnp.int32, (r, width), 1) >> shift
    zero = jnp.zeros_like(x)
    return jnp.concatenate([jnp.where(blk == a, x, zero) for a in range(n_blocks)], axis=0)


def _delta_prep_kernel(q_ref, k_ref, v_ref, col_ref, row_ref, w_ref, u_ref, qg_ref, kd_ref, qk_ref, egl_ref):
    width = GROUP * CHUNK
    ri = lax.broadcasted_iota(jnp.int32, (CHUNK, width), 0)
    cj = lax.broadcasted_iota(jnp.int32, (CHUNK, width), 1) & (CHUNK - 1)
    low_half = lax.broadcasted_iota(jnp.int32, (CHUNK, 128), 1) < CHUNK
    eye = (ri == cj).astype(F32)
    masks = ((ri >= cj, ri > cj, CHUNK - 1), (ri <= cj, ri < cj, 0))

    def side_by_side(parts):
        return jnp.concatenate([jnp.where(low_half, parts[0], parts[1]), jnp.where(low_half, parts[2], parts[3])],
                               axis=1)

    gram, kbd = {}, {}
    for c in range(PREP_CHUNKS):
        rows = slice(c * CHUNK, (c + 1) * CHUNK)
        for g in range(N_GROUPS):
            cols = slice(g * 512, (g + 1) * 512)
            k4 = k_ref[rows, cols]
            kbd[c, g] = _block_diag(k4, GROUP)
            gram[c, g] = _dot_nt(jnp.concatenate([q_ref[rows, cols], k4], axis=0), kbd[c, g])

    x, p = {}, {}
    for c in range(PREP_CHUNKS):
        rows = slice(c * CHUNK, (c + 1) * CHUNK)
        for d, (incl, strict, last) in enumerate(masks):
            for g in range(N_GROUPS):
                beta_cols, gc_cols = [], []
                for a in range(GROUP):
                    h = g * GROUP + a
                    idx = d * B_HEADS + h
                    hcols = slice(h * 128, (h + 1) * 128)
                    beta_b = jnp.broadcast_to(col_ref[rows, idx:idx + 1], (CHUNK, 128))
                    gc_b = jnp.broadcast_to(col_ref[rows, 2 * B_HEADS + idx:2 * B_HEADS + idx + 1], (CHUNK, 128))
                    g_last = row_ref[c, 2 + d:3 + d, h * CHUNK + last:h * CHUNK + last + 1]
                    qg_ref[d, rows, hcols] = (q_ref[rows, hcols].astype(F32) * jnp.exp(gc_b)).astype(BF16)
                    kd_ref[d, rows, hcols] = (k_ref[rows, hcols].astype(F32) * jnp.exp(g_last - gc_b)).astype(BF16)
                    egl_ref[d, c, :, hcols] = jnp.exp(jnp.broadcast_to(g_last, (1, 128)))
                    beta_cols.append(beta_b)
                    gc_cols.append(gc_b)
                lanes = slice(g * width, (g + 1) * width)
                gc_row = row_ref[c, 2 + d:3 + d, lanes]
                decay = jnp.where(incl, jnp.exp(jnp.where(incl, side_by_side(gc_cols) - gc_row, 0.0)), 0.0)
                gm = gram[c, g]
                qk_ref[d, rows, lanes] = (gm[:CHUNK] * decay).astype(BF16)
                low = jnp.where(strict, gm[CHUNK:] * decay * side_by_side(beta_cols), 0.0)
                x[c, d, g] = -low
                p[c, d, g] = eye - low

    keys = list(x)
    for key in keys:
        xb = x[key].astype(BF16)
        x[key] = _dot(xb, _block_diag(xb, GROUP))
    squarings = CHUNK.bit_length() - 2
    for _ in range(squarings - 1):
        for key in keys:
            xb = x[key].astype(BF16)
            z = _dot(jnp.concatenate([xb, p[key].astype(BF16)], axis=0), _block_diag(xb, GROUP))
            x[key] = z[:CHUNK]
            p[key] = p[key] + z[CHUNK:]
    for key in keys:
        p[key] = p[key] + _dot(p[key].astype(BF16), _block_diag(x[key].astype(BF16), GROUP))

    for c in range(PREP_CHUNKS):
        rows = slice(c * CHUNK, (c + 1) * CHUNK)
        for g in range(N_GROUPS):
            cols = slice(g * 512, (g + 1) * 512)
            lanes = slice(g * width, (g + 1) * width)
            tb = [p[c, d, g] * row_ref[c, d:d + 1, lanes] for d in range(2)]
            tbg = [tb[d] * jnp.exp(row_ref[c, 2 + d:3 + d, lanes]) for d in range(2)]
            uu = _dot(jnp.concatenate(tb, axis=0).astype(BF16), _block_diag(v_ref[rows, cols], GROUP))
            ww = _dot(jnp.concatenate(tbg, axis=0).astype(BF16), kbd[c, g])
            for d in range(2):
                u_ref[d, rows, cols] = uu[d * CHUNK:(d + 1) * CHUNK].astype(BF16)
                w_ref[d, rows, cols] = ww[d * CHUNK:(d + 1) * CHUNK].astype(BF16)


def _delta_prep(q, k, v, col, row):
    b, s, kw = q.shape
    nc = s // CHUNK
    tp = PREP_CHUNKS * CHUNK
    assert s % tp == 0
    row4 = row.reshape(b, 4, B_HEADS, nc, CHUNK).transpose(0, 3, 1, 2, 4).reshape(b, nc, 4, B_HEADS * CHUNK)
    tile = pl.BlockSpec((None, tp, kw), lambda i, j: (i, j, 0))
    both = lambda n: pl.BlockSpec((None, 2, tp, n), lambda i, j: (i, 0, j, 0))
    act = jax.ShapeDtypeStruct((b, 2, s, kw), BF16)
    return pl.pallas_call(
        _delta_prep_kernel,
        grid=(b, s // tp),
        in_specs=[tile, tile, tile,
                  pl.BlockSpec((None, tp, 4 * B_HEADS), lambda i, j: (i, j, 0)),
                  pl.BlockSpec((None, PREP_CHUNKS, 4, B_HEADS * CHUNK), lambda i, j: (i, j, 0, 0))],
        out_specs=[both(kw), both(kw), both(kw), both(kw), both(B_HEADS * CHUNK),
                   pl.BlockSpec((None, 2, PREP_CHUNKS, 1, kw), lambda i, j: (i, 0, j, 0, 0))],
        out_shape=[act, act, act, act, jax.ShapeDtypeStruct((b, 2, s, B_HEADS * CHUNK), BF16),
                   jax.ShapeDtypeStruct((b, 2, nc, 1, kw), F32)],
        compiler_params=_params(("parallel", "parallel")),
        name="delta_prep",
    )(q, k, v, col, row4)


def _delta_scan_kernel(*refs):
    per_dir = 6
    dirs = (refs[:per_dir] + (refs[2 * per_dir],), refs[per_dir:2 * per_dir] + (refs[2 * per_dir + 1],))
    state_ref = refs[2 * per_dir + 2]

    @pl.when(pl.program_id(1) == 0)
    def _():
        state_ref[...] = jnp.zeros_like(state_ref)

    tile = PAIR * B_HEAD_K
    on_diag = ((lax.broadcasted_iota(jnp.int32, (tile, tile), 0) >> 7)
               == (lax.broadcasted_iota(jnp.int32, (tile, tile), 1) >> 7))
    units = [(d, hp) for d in range(2) for hp in range(N_PAIRS)]
    rs = {}
    for d, hp in units:
        w_ref, _, qg_ref = dirs[d][:3]
        cols = slice(hp * tile, (hp + 1) * tile)
        lhs = jnp.concatenate([w_ref[:, cols], qg_ref[:, cols]], axis=0)
        rs[d, hp] = _dot(lhs, state_ref[d * N_PAIRS + hp].astype(BF16))
    for d, hp in units:
        _, u_ref, _, kd_ref, qk_ref, egl_ref, o_ref = dirs[d]
        cols = slice(hp * tile, (hp + 1) * tile)
        r = rs[d, hp]
        v_new = (u_ref[:, cols].astype(F32) - r[:CHUNK]).astype(BF16)
        qk = qk_ref[:, hp * PAIR * CHUNK:(hp + 1) * PAIR * CHUNK]
        o_ref[:, cols] = (r[CHUNK:] + _dot(qk, _block_diag(v_new, PAIR))).astype(o_ref.dtype)
        update = _dot_tn(kd_ref[:, cols], v_new)
        idx = d * N_PAIRS + hp
        state_ref[idx] = state_ref[idx] * egl_ref[:, cols] + jnp.where(on_diag, update, 0.0)


def _delta_scan(w, u, qg, kd, qk, egl):
    b, _, s, kw = w.shape
    nc = s // CHUNK
    chunk_of = (lambda c: c, lambda c: nc - 1 - c)
    in_specs = []
    for d in range(2):
        act = lambda n, d=d: pl.BlockSpec((None, None, CHUNK, n), lambda i, c: (i, d, chunk_of[d](c), 0))
        in_specs += [act(kw), act(kw), act(kw), act(kw), act(B_HEADS * CHUNK),
                     pl.BlockSpec((None, None, None, 1, kw), lambda i, c, d=d: (i, d, chunk_of[d](c), 0, 0))]
    out = jax.ShapeDtypeStruct((b, s, kw), BF16)
    out_specs = [pl.BlockSpec((None, CHUNK, kw), lambda i, c, d=d: (i, chunk_of[d](c), 0)) for d in range(2)]
    args = (w, u, qg, kd, qk, egl)
    return pl.pallas_call(
        _delta_scan_kernel,
        grid=(b, nc),
        in_specs=in_specs,
        out_specs=out_specs,
        out_shape=[out, out],
        scratch_shapes=[pltpu.VMEM((2 * N_PAIRS, PAIR * B_HEAD_K, PAIR * B_HEAD_V), F32)],
        compiler_params=_params(("parallel", "arbitrary")),
        name="delta_scan",
    )(*args, *args)


def _delta_out_kernel(of_ref, ob_ref, gate_ref, x_ref, onorm_ref, w_ref, fnorm_ref, o_ref, og_scr):
    onorm = onorm_ref[...]
    for h in range(B_HEADS):
        cols = slice(h * 128, (h + 1) * 128)
        o = of_ref[:, cols].astype(F32) + ob_ref[:, cols].astype(F32)
        o = o * lax.rsqrt(jnp.mean(o * o, axis=-1, keepdims=True) + EPS) * onorm
        gate = gate_ref[:, cols].astype(F32)
        og_scr[:, cols] = (o * (gate * _sigmoid(gate))).astype(BF16)
    y = x_ref[...] + _dot(og_scr[...], w_ref[...])
    o_ref[...] = _rms_rows(y, fnorm_ref[...])


def _delta_out(o_f, o_b, gate, x1, out_norm, w_out, final_norm):
    t = x1.shape[0]
    tm = 512
    assert t % tm == 0
    row = pl.BlockSpec((tm, D_MODEL), lambda i: (i, 0))
    vec = lambda n: pl.BlockSpec((1, n), lambda i: (0, 0))
    return pl.pallas_call(
        _delta_out_kernel,
        grid=(t // tm,),
        in_specs=[row, row, row, row, vec(B_HEAD_V), pl.BlockSpec(w_out.shape, lambda i: (0, 0)), vec(D_MODEL)],
        out_specs=row,
        out_shape=jax.ShapeDtypeStruct((t, D_MODEL), F32),
        scratch_shapes=[pltpu.VMEM((tm, D_MODEL), BF16)],
        compiler_params=_params(("parallel",)),
        name="delta_out",
    )(o_f, o_b, gate, x1, out_norm, w_out, final_norm)


def _attn_weights(w_in):
    kvw = A_KV_HEADS * A_HEAD_DIM
    wq = w_in[:, :D_MODEL]
    wk = w_in[:, D_MODEL:D_MODEL + kvw].reshape(D_MODEL, A_KV_HEADS, 1, A_HEAD_DIM)
    wv = w_in[:, D_MODEL + kvw:D_MODEL + 2 * kvw].reshape(D_MODEL, A_KV_HEADS, 1, A_HEAD_DIM)
    rep = lambda w: jnp.broadcast_to(w, (D_MODEL, A_KV_HEADS, 2, A_HEAD_DIM)).reshape(D_MODEL, 2 * kvw)
    wg = w_in[:, D_MODEL + 2 * kvw:]
    return jnp.concatenate([wq, rep(wk), rep(wv), wg], axis=1).astype(BF16)


def kernel(x, attn_norm, attn_w_in, attn_sink, attn_w_out, delta_norm, delta_w_in, delta_conv, delta_a_log,
           delta_dt_bias, delta_out_norm, delta_w_out, final_norm):
    b, s, d = x.shape
    assert d == D_MODEL
    t = b * s
    q, kv, gate = _attn_proj(x.reshape(t, d), attn_norm.reshape(1, d).astype(F32), _attn_weights(attn_w_in))
    shape3 = (b, s, d)
    x1 = _attn_core(x, q.reshape(shape3), kv.reshape(shape3), gate.reshape(shape3), attn_sink.astype(F32),
                    attn_w_out.astype(BF16))
    dq, dk, dv, dgate, col, row = _delta_proj(x1, delta_norm.reshape(1, d).astype(F32), delta_w_in, delta_conv,
                                              delta_a_log, delta_dt_bias)
    o_f, o_b = _delta_scan(*_delta_prep(dq, dk, dv, col, row))
    out = _delta_out(o_f.reshape(t, d), o_b.reshape(t, d), dgate.reshape(t, d), x1.reshape(t, d),
                     delta_out_norm.reshape(1, B_HEAD_V).astype(F32), delta_w_out.astype(BF16),
                     final_norm.reshape(1, d).astype(F32))
    return out.reshape(b, s, d)
```

```python
import functools

import numpy as np
import jax
import jax.numpy as jnp
from jax import lax
from jax.experimental import pallas as pl
from jax.experimental.pallas import tpu as pltpu

F32 = jnp.float32
BF16 = jnp.bfloat16
EPS = 1e-6
LOG2E = float(np.log2(np.e))

D_MODEL = 1024
A_HEADS = 16
A_KV_HEADS = 4
A_HEAD_DIM = 64
A_PAIRS = A_HEADS // 2
BLOCK = 128
KEYS = 3 * BLOCK

B_HEADS = 8
B_HEAD_K = 128
B_HEAD_V = 128
CONV_WIDTH = 5
CONV_PAD = CONV_WIDTH // 2
CHUNK = 64
N_QKV = 3 * B_HEADS * B_HEAD_K
HALO = 8
PROJ_BLOCK = 256

V7X_VMEM_LIMIT_BYTES = 56 * 1024 * 1024


def _sigmoid(x):
    return 1.0 / (1.0 + jnp.exp(-x))


def _softplus(x):
    return jnp.maximum(x, 0.0) + jnp.log(1.0 + jnp.exp(-jnp.abs(x)))


def _rms_rows(x, gain):
    ms = jnp.mean(x * x, axis=-1, keepdims=True)
    return x * lax.rsqrt(ms + EPS) * gain


def _split3(x):
    a = x.astype(BF16)
    r = x - a.astype(F32)
    b = r.astype(BF16)
    c = (r - b.astype(F32)).astype(BF16)
    return a, b, c


def _dot(a, b):
    return jnp.dot(a, b, preferred_element_type=F32)


def _dot_nt(a, b):
    return lax.dot_general(a, b, (((1,), (1,)), ((), ())), preferred_element_type=F32)


def _dot_tn(a, b):
    return lax.dot_general(a, b, (((0,), (0,)), ((), ())), preferred_element_type=F32)


def _params(semantics):
    return pltpu.CompilerParams(dimension_semantics=semantics, vmem_limit_bytes=V7X_VMEM_LIMIT_BYTES)


def _attn_proj_kernel(x_ref, gain_ref, w_ref, q_ref, kv_ref, gate_ref):
    hn = _rms_rows(x_ref[...], gain_ref[...]).astype(BF16)
    proj = _dot(hn, w_ref[...])
    q_ref[...] = (proj[:, :D_MODEL] * (A_HEAD_DIM ** -0.5 * LOG2E)).astype(BF16)
    kv_ref[...] = proj[:, D_MODEL:2 * D_MODEL].astype(BF16)
    gate_ref[...] = proj[:, 2 * D_MODEL:].astype(BF16)


def _attn_proj(x2, gain, w):
    t = x2.shape[0]
    tm = 512
    assert t % tm == 0
    out = jax.ShapeDtypeStruct((t, D_MODEL), BF16)
    row = pl.BlockSpec((tm, D_MODEL), lambda i: (i, 0))
    return pl.pallas_call(
        _attn_proj_kernel,
        grid=(t // tm,),
        in_specs=[row, pl.BlockSpec((1, D_MODEL), lambda i: (0, 0)), pl.BlockSpec(w.shape, lambda i: (0, 0))],
        out_specs=[row, row, row],
        out_shape=[out, out, out],
        compiler_params=_params(("parallel",)),
        name="attn_proj",
    )(x2, gain, w)


def _attn_core_kernel(sink_ref, q_ref, kvp_ref, kvc_ref, kvn_ref, gate_ref, x_ref, bias_ref, w_ref,
                      o_ref, kv_scr, *, tq, n_tiles):
    j = pl.program_id(1)
    kv_scr[0:BLOCK, :] = kvp_ref[...]
    kv_scr[BLOCK:BLOCK + tq, :] = kvc_ref[...]
    kv_scr[BLOCK + tq:, :] = kvn_ref[...]
    low = lax.broadcasted_iota(jnp.int32, (KEYS, 2 * A_HEAD_DIM), 1) < A_HEAD_DIM
    low_q = lax.broadcasted_iota(jnp.int32, (BLOCK, 2 * A_HEAD_DIM), 1) < A_HEAD_DIM
    zero = jnp.zeros((KEYS, 2 * A_HEAD_DIM), BF16)
    first_head_rows = lax.broadcasted_iota(jnp.int32, (2 * KEYS, 2 * A_HEAD_DIM), 0) < KEYS
    first_head_lanes = lax.broadcasted_iota(jnp.int32, (2 * KEYS, 2 * A_HEAD_DIM), 1) < A_HEAD_DIM
    ones_bd = jnp.where(first_head_rows == first_head_lanes, 1.0, 0.0).astype(BF16)
    n_qb = tq // BLOCK

    k_bd, v_ext = {}, {}

    def scores(qb, pair):
        rows = slice(qb * BLOCK, qb * BLOCK + KEYS)
        g = pair // 2
        if (qb, g) not in k_bd:
            k_rep = kv_scr[rows, g * 128:(g + 1) * 128]
            v_rep = kv_scr[rows, 512 + g * 128:512 + (g + 1) * 128]
            k_bd[qb, g] = jnp.concatenate([jnp.where(low, k_rep, zero), jnp.where(low, zero, k_rep)], axis=0)
            v_bd = jnp.concatenate([jnp.where(low, v_rep, zero), jnp.where(low, zero, v_rep)], axis=0)
            v_ext[qb, g] = jnp.concatenate([v_bd, ones_bd], axis=1)
        gqb = j * n_qb + qb
        table = jnp.where(gqb == 0, 1, jnp.where(gqb == n_tiles * n_qb - 1, 2, 0)) * A_PAIRS
        q_pair = q_ref[qb * BLOCK:(qb + 1) * BLOCK, pair * 128:(pair + 1) * 128]
        return _dot_nt(q_pair, k_bd[qb, g]) + bias_ref[table + pair]

    def out_chunk(qb, og_block, c):
        rows = slice(qb * BLOCK, (qb + 1) * BLOCK)
        cols = slice(c * 256, (c + 1) * 256)
        o_ref[rows, cols] = x_ref[rows, cols] + _dot(og_block, w_ref[:, cols])

    items = [(qb, pair) for qb in range(n_qb) for pair in range(A_PAIRS)]
    pending = []
    og_parts = []
    s_next = scores(*items[0])
    for n, (qb, pair) in enumerate(items):
        rows = slice(qb * BLOCK, (qb + 1) * BLOCK)
        s = s_next
        if n + 1 < len(items):
            s_next = scores(*items[n + 1])
        if pending and pair % 2 == 1:
            pending.pop(0)()
        probs, sink_mass = [], []
        for hh in range(2):
            sink = sink_ref[2 * pair + hh]
            sh = s[:, hh * KEYS:(hh + 1) * KEYS]
            m = jnp.maximum(jnp.max(sh, axis=-1, keepdims=True), sink)
            probs.append(jnp.exp2(sh - m).astype(BF16))
            sink_mass.append(jnp.exp2(sink - m))
        o_ext = _dot(jnp.concatenate(probs, axis=1), v_ext[qb, pair // 2])
        denom = o_ext[:, 128:] + jnp.where(low_q, sink_mass[0], sink_mass[1])
        gate = gate_ref[rows, pair * 128:(pair + 1) * 128].astype(F32)
        og_parts.append((o_ext[:, :128] * (1.0 / denom) * (gate * _sigmoid(gate))).astype(BF16))
        if pair == A_PAIRS - 1:
            og_block = jnp.concatenate(og_parts, axis=1)
            og_parts = []
            pending += [functools.partial(out_chunk, qb, og_block, c) for c in range(D_MODEL // 256)]
    for piece in pending:
        piece()


def _attn_tables():
    qpos = np.arange(BLOCK)[:, None]
    krel = np.arange(KEYS)[None, :] - BLOCK
    dist = np.abs(krel - qpos).astype(np.float32)
    slopes = np.power(2.0, -8.0 * np.arange(1, A_HEADS + 1) / A_HEADS).astype(np.float32)
    band = np.where(dist <= BLOCK, 0.0, -np.inf)
    per_head = (-slopes[:, None, None] * dist[None]).astype(np.float64) * LOG2E + band[None]
    sets = np.broadcast_to(per_head[None], (3, A_HEADS, BLOCK, KEYS)).copy()
    sets[1, :, :, :BLOCK] = -np.inf
    sets[2, :, :, 2 * BLOCK:] = -np.inf
    sets = sets.reshape(3, A_PAIRS, 2, BLOCK, KEYS).transpose(0, 1, 3, 2, 4)
    return jnp.asarray(sets.reshape(3 * A_PAIRS, BLOCK, 2 * KEYS), F32)


def _attn_core(x, q, kv, gate, sink, w_out):
    b, s, _ = x.shape
    tq = 512
    assert s % tq == 0 and s // BLOCK >= 2
    n_tiles = s // tq
    nb = s // BLOCK
    per = tq // BLOCK
    bias = _attn_tables()
    tile = pl.BlockSpec((None, tq, D_MODEL), lambda i, j: (i, j, 0))
    kernel = functools.partial(_attn_core_kernel, tq=tq, n_tiles=n_tiles)
    return pl.pallas_call(
        kernel,
        grid=(b, n_tiles),
        in_specs=[
            pl.BlockSpec(memory_space=pltpu.SMEM),
            tile,
            pl.BlockSpec((None, BLOCK, D_MODEL), lambda i, j: (i, jnp.maximum(j * per - 1, 0), 0)),
            tile,
            pl.BlockSpec((None, BLOCK, D_MODEL), lambda i, j: (i, jnp.minimum((j + 1) * per, nb - 1), 0)),
            tile,
            tile,
            pl.BlockSpec(bias.shape, lambda i, j: (0, 0, 0), pipeline_mode=pl.Buffered(1)),
            pl.BlockSpec(w_out.shape, lambda i, j: (0, 0), pipeline_mode=pl.Buffered(1)),
        ],
        out_specs=tile,
        out_shape=jax.ShapeDtypeStruct(x.shape, F32),
        scratch_shapes=[pltpu.VMEM((tq + 2 * BLOCK, D_MODEL), BF16)],
        compiler_params=_params(("parallel", "parallel")),
        name="attn_core",
    )(sink * LOG2E, q, kv, kv, kv, gate, x, bias, w_out)


def _delta_proj_kernel(xp_ref, xc_ref, xn_ref, gain_ref, wqkv_ref, wg_ref, wba_ref, wbat_ref, conv_ref,
                       alog_ref, dtb_ref, alogc_ref, dtbc_ref, tril_ref, triu_ref, trilt_ref, triut_ref,
                       q_ref, k_ref, v_ref, gate_ref, col_ref, row_ref, hn_scr, proj_scr, *, tm, n_tiles):
    j = pl.program_id(1)
    gain = gain_ref[...]
    keep_prev = (j > 0).astype(F32)
    keep_next = (j < n_tiles - 1).astype(F32)
    hn_scr[0:HALO, :] = (_rms_rows(xp_ref[...], gain) * keep_prev).astype(BF16)
    hn_scr[HALO:HALO + tm, :] = _rms_rows(xc_ref[...], gain).astype(BF16)
    hn_scr[HALO + tm:, :] = (_rms_rows(xn_ref[...], gain) * keep_next).astype(BF16)
    hn = hn_scr[HALO:HALO + tm, :]
    gate_ref[...] = _dot(hn, wg_ref[...]).astype(BF16)

    ba = _dot(hn, wba_ref[...])
    g_c = -jnp.exp(alog_ref[...]) * _softplus(ba + dtb_ref[...])
    bat = _dot_nt(wbat_ref[...], hn)
    g_r = -jnp.exp(alogc_ref[...]) * _softplus(bat + dtbc_ref[...])
    gc3 = _split3(g_c)
    gr3 = _split3(g_r)
    ent_r = lax.broadcasted_iota(jnp.int32, (4 * B_HEADS, tm), 0)
    cum_r = jnp.where(ent_r < 3 * B_HEADS, sum(_dot(p, trilt_ref[...]) for p in gr3),
                      sum(_dot(p, triut_ref[...]) for p in gr3))
    row_ref[...] = jnp.where(ent_r < 2 * B_HEADS, _sigmoid(bat), cum_r)
    ent_c = lax.broadcasted_iota(jnp.int32, (CHUNK, 4 * B_HEADS), 1)
    beta_c = _sigmoid(ba)
    for c in range(tm // CHUNK):
        rows = slice(c * CHUNK, (c + 1) * CHUNK)
        fwd = sum(_dot(tril_ref[...], p[rows]) for p in gc3)
        bwd = sum(_dot(triu_ref[...], p[rows]) for p in gc3)
        col_ref[rows, :] = jnp.where(ent_c < 2 * B_HEADS, beta_c[rows], jnp.where(ent_c < 3 * B_HEADS, fwd, bwd))

    taps = conv_ref[...]
    hn_ext = hn_scr[...]
    n_blocks = N_QKV // PROJ_BLOCK
    rc = 128

    def project(n):
        proj_scr[n] = _dot(hn_ext, wqkv_ref[:, n * PROJ_BLOCK:(n + 1) * PROJ_BLOCK])

    def conv(n):
        for half in range(PROJ_BLOCK // 128):
            cb = n * (PROJ_BLOCK // 128) + half
            cols = slice(cb * 128, (cb + 1) * 128)
            out_ref, head = ((q_ref, k_ref, v_ref)[cb // B_HEADS], cb % B_HEADS)
            for r0 in range(0, tm, rc):
                window = proj_scr[n, r0:r0 + rc + 2 * HALO, half * 128:(half + 1) * 128]
                acc = None
                for t in range(CONV_WIDTH):
                    shift = CONV_PAD - t
                    moved = window if shift == 0 else pltpu.roll(window, shift % (rc + 2 * HALO), axis=0)
                    term = taps[t:t + 1, cols] * moved[HALO:HALO + rc]
                    acc = term if acc is None else acc + term
                y = acc * _sigmoid(acc)
                if out_ref is not v_ref:
                    scale = 1.0 if out_ref is k_ref else B_HEAD_K ** -0.5
                    y = y * (lax.rsqrt(jnp.sum(y * y, axis=-1, keepdims=True) + EPS) * scale)
                out_ref[r0:r0 + rc, head * 128:(head + 1) * 128] = y.astype(BF16)

    project(0)
    for n in range(n_blocks):
        if n + 1 < n_blocks:
            project(n + 1)
        conv(n)


def _delta_proj(x1, gain, w_in, conv_w, a_log, dt_bias):
    b, s, _ = x1.shape
    tm = 512
    assert s % tm == 0
    n_tiles = s // tm
    per = tm // HALO
    nh = s // HALO
    kw = B_HEADS * B_HEAD_K
    wqkv = w_in[:, :N_QKV].astype(BF16)
    wg = w_in[:, N_QKV:N_QKV + kw].astype(BF16)
    wba = w_in[:, N_QKV + kw:].astype(BF16)
    wbat = wba.T
    pad = jnp.zeros((1, 2 * B_HEADS), F32)
    alog = jnp.concatenate([pad, a_log.reshape(1, 2 * B_HEADS).astype(F32)], axis=1)
    dtb = jnp.concatenate([pad, dt_bias.reshape(1, 2 * B_HEADS).astype(F32)], axis=1)
    i = np.arange(CHUNK)
    tril = (i[:, None] >= i[None, :]).astype(np.float32)
    triu = tril.T
    eye = np.eye(tm // CHUNK, dtype=np.float32)
    trilt = jnp.asarray(np.kron(eye, tril.T), BF16)
    triut = jnp.asarray(np.kron(eye, triu.T), BF16)
    tril = jnp.asarray(tril, BF16)
    triu = jnp.asarray(triu, BF16)
    full = lambda a: pl.BlockSpec(a.shape, lambda i_, j_: (0,) * a.ndim)
    tile = pl.BlockSpec((None, tm, D_MODEL), lambda i_, j_: (i_, j_, 0))
    act = jax.ShapeDtypeStruct((b, s, kw), BF16)
    kernel = functools.partial(_delta_proj_kernel, tm=tm, n_tiles=n_tiles)
    consts = [gain, wqkv, wg, wba, wbat, conv_w.astype(F32), alog, dtb, alog.T, dtb.T, tril, triu, trilt, triut]
    return pl.pallas_call(
        kernel,
        grid=(b, n_tiles),
        in_specs=[
            pl.BlockSpec((None, HALO, D_MODEL), lambda i_, j_: (i_, jnp.maximum(j_ * per - 1, 0), 0)),
            tile,
            pl.BlockSpec((None, HALO, D_MODEL), lambda i_, j_: (i_, jnp.minimum((j_ + 1) * per, nh - 1), 0)),
        ] + [full(a) for a in consts],
        out_specs=[tile, tile, tile, tile,
                   pl.BlockSpec((None, tm, 4 * B_HEADS), lambda i_, j_: (i_, j_, 0)),
                   pl.BlockSpec((None, 4 * B_HEADS, tm), lambda i_, j_: (i_, 0, j_))],
        out_shape=[act, act, act, act,
                   jax.ShapeDtypeStruct((b, s, 4 * B_HEADS), F32),
                   jax.ShapeDtypeStruct((b, 4 * B_HEADS, s), F32)],
        scratch_shapes=[pltpu.VMEM((tm + 2 * HALO, D_MODEL), BF16),
                        pltpu.VMEM((N_QKV // PROJ_BLOCK, tm + 2 * HALO, PROJ_BLOCK), F32)],
        compiler_params=_params(("parallel", "parallel")),
        name="delta_proj",
    )(x1, x1, x1, *consts)


PREP_CHUNKS = 8
GROUP = 4
N_GROUPS = B_HEADS // GROUP
PAIR = 2
N_PAIRS = B_HEADS // PAIR


def _block_diag(x, n_blocks):
    r, width = x.shape
    shift = (width // n_blocks).bit_length() - 1
    blk = lax.broadcasted_iota(jnp.int32, (r, width), 1) >> shift
    zero = jnp.zeros_like(x)
    return jnp.concatenate([jnp.where(blk == a, x, zero) for a in range(n_blocks)], axis=0)


def _delta_prep_kernel(q_ref, k_ref, v_ref, col_ref, row_ref, w_ref, u_ref, qg_ref, kd_ref, qk_ref, egl_ref):
    width = GROUP * CHUNK
    ri = lax.broadcasted_iota(jnp.int32, (CHUNK, width), 0)
    cj = lax.broadcasted_iota(jnp.int32, (CHUNK, width), 1) & (CHUNK - 1)
    low_half = lax.broadcasted_iota(jnp.int32, (CHUNK, 128), 1) < CHUNK
    eye = (ri == cj).astype(F32)
    masks = ((ri >= cj, ri > cj, CHUNK - 1), (ri <= cj, ri < cj, 0))

    def side_by_side(parts):
        return jnp.concatenate([jnp.where(low_half, parts[0], parts[1]), jnp.where(low_half, parts[2], parts[3])],
                               axis=1)

    gram, kbd = {}, {}
    for c in range(PREP_CHUNKS):
        rows = slice(c * CHUNK, (c + 1) * CHUNK)
        for g in range(N_GROUPS):
            cols = slice(g * 512, (g + 1) * 512)
            k4 = k_ref[rows, cols]
            kbd[c, g] = _block_diag(k4, GROUP)
            gram[c, g] = _dot_nt(jnp.concatenate([q_ref[rows, cols], k4], axis=0), kbd[c, g])

    x, p = {}, {}
    for c in range(PREP_CHUNKS):
        rows = slice(c * CHUNK, (c + 1) * CHUNK)
        for d, (incl, strict, last) in enumerate(masks):
            for g in range(N_GROUPS):
                beta_cols, gc_cols = [], []
                for a in range(GROUP):
                    h = g * GROUP + a
                    idx = d * B_HEADS + h
                    hcols = slice(h * 128, (h + 1) * 128)
                    beta_b = jnp.broadcast_to(col_ref[rows, idx:idx + 1], (CHUNK, 128))
                    gc_b = jnp.broadcast_to(col_ref[rows, 2 * B_HEADS + idx:2 * B_HEADS + idx + 1], (CHUNK, 128))
                    g_last = row_ref[c, 2 + d:3 + d, h * CHUNK + last:h * CHUNK + last + 1]
                    qg_ref[d, rows, hcols] = (q_ref[rows, hcols].astype(F32) * jnp.exp(gc_b)).astype(BF16)
                    kd_ref[d, rows, hcols] = (k_ref[rows, hcols].astype(F32) * jnp.exp(g_last - gc_b)).astype(BF16)
                    egl_ref[d, c, :, hcols] = jnp.exp(jnp.broadcast_to(g_last, (1, 128)))
                    beta_cols.append(beta_b)
                    gc_cols.append(gc_b)
                lanes = slice(g * width, (g + 1) * width)
                gc_row = row_ref[c, 2 + d:3 + d, lanes]
                decay = jnp.where(incl, jnp.exp(jnp.where(incl, side_by_side(gc_cols) - gc_row, 0.0)), 0.0)
                gm = gram[c, g]
                qk_ref[d, rows, lanes] = (gm[:CHUNK] * decay).astype(BF16)
                low = jnp.where(strict, gm[CHUNK:] * decay * side_by_side(beta_cols), 0.0)
                x[c, d, g] = -low
                p[c, d, g] = eye - low

    keys = list(x)
    for key in keys:
        xb = x[key].astype(BF16)
        x[key] = _dot(xb, _block_diag(xb, GROUP))
    squarings = CHUNK.bit_length() - 2
    for _ in range(squarings - 1):
        for key in keys:
            xb = x[key].astype(BF16)
            z = _dot(jnp.concatenate([xb, p[key].astype(BF16)], axis=0), _block_diag(xb, GROUP))
            x[key] = z[:CHUNK]
            p[key] = p[key] + z[CHUNK:]
    for key in keys:
        p[key] = p[key] + _dot(p[key].astype(BF16), _block_diag(x[key].astype(BF16), GROUP))

    for c in range(PREP_CHUNKS):
        rows = slice(c * CHUNK, (c + 1) * CHUNK)
        for g in range(N_GROUPS):
            cols = slice(g * 512, (g + 1) * 512)
            lanes = slice(g * width, (g + 1) * width)
            tb = [p[c, d, g] * row_ref[c, d:d + 1, lanes] for d in range(2)]
            tbg = [tb[d] * jnp.exp(row_ref[c, 2 + d:3 + d, lanes]) for d in range(2)]
            uu = _dot(jnp.concatenate(tb, axis=0).astype(BF16), _block_diag(v_ref[rows, cols], GROUP))
            ww = _dot(jnp.concatenate(tbg, axis=0).astype(BF16), kbd[c, g])
            for d in range(2):
                u_ref[d, rows, cols] = uu[d * CHUNK:(d + 1) * CHUNK].astype(BF16)
                w_ref[d, rows, cols] = ww[d * CHUNK:(d + 1) * CHUNK].astype(BF16)


def _delta_prep(q, k, v, col, row):
    b, s, kw = q.shape
    nc = s // CHUNK
    tp = PREP_CHUNKS * CHUNK
    assert s % tp == 0
    row4 = row.reshape(b, 4, B_HEADS, nc, CHUNK).transpose(0, 3, 1, 2, 4).reshape(b, nc, 4, B_HEADS * CHUNK)
    tile = pl.BlockSpec((None, tp, kw), lambda i, j: (i, j, 0))
    both = lambda n: pl.BlockSpec((None, 2, tp, n), lambda i, j: (i, 0, j, 0))
    act = jax.ShapeDtypeStruct((b, 2, s, kw), BF16)
    return pl.pallas_call(
        _delta_prep_kernel,
        grid=(b, s // tp),
        in_specs=[tile, tile, tile,
                  pl.BlockSpec((None, tp, 4 * B_HEADS), lambda i, j: (i, j, 0)),
                  pl.BlockSpec((None, PREP_CHUNKS, 4, B_HEADS * CHUNK), lambda i, j: (i, j, 0, 0))],
        out_specs=[both(kw), both(kw), both(kw), both(kw), both(B_HEADS * CHUNK),
                   pl.BlockSpec((None, 2, PREP_CHUNKS, 1, kw), lambda i, j: (i, 0, j, 0, 0))],
        out_shape=[act, act, act, act, jax.ShapeDtypeStruct((b, 2, s, B_HEADS * CHUNK), BF16),
                   jax.ShapeDtypeStruct((b, 2, nc, 1, kw), F32)],
        compiler_params=_params(("parallel", "parallel")),
        name="delta_prep",
    )(q, k, v, col, row4)


def _delta_scan_kernel(*refs):
    per_dir = 6
    dirs = (refs[:per_dir] + (refs[2 * per_dir],), refs[per_dir:2 * per_dir] + (refs[2 * per_dir + 1],))
    state_ref = refs[2 * per_dir + 2]
    n_batch = dirs[0][0].shape[0]

    @pl.when(pl.program_id(0) == 0)
    def _():
        state_ref[...] = jnp.zeros_like(state_ref)

    tile = PAIR * B_HEAD_K
    on_diag = ((lax.broadcasted_iota(jnp.int32, (tile, tile), 0) >> 7)
               == (lax.broadcasted_iota(jnp.int32, (tile, tile), 1) >> 7))
    units = [(i, d, hp) for i in range(n_batch) for d in range(2) for hp in range(N_PAIRS)]
    slot = lambda i, d, hp: (i * 2 + d) * N_PAIRS + hp
    rs = {}
    for i, d, hp in units:
        w_ref, _, qg_ref = dirs[d][:3]
        cols = slice(hp * tile, (hp + 1) * tile)
        lhs = jnp.concatenate([w_ref[i, :, cols], qg_ref[i, :, cols]], axis=0)
        rs[i, d, hp] = _dot(lhs, state_ref[slot(i, d, hp)].astype(BF16))
    for i, d, hp in units:
        _, u_ref, _, kd_ref, qk_ref, egl_ref, o_ref = dirs[d]
        cols = slice(hp * tile, (hp + 1) * tile)
        r = rs[i, d, hp]
        v_new = (u_ref[i, :, cols].astype(F32) - r[:CHUNK]).astype(BF16)
        qk = qk_ref[i, :, hp * PAIR * CHUNK:(hp + 1) * PAIR * CHUNK]
        o_ref[i, :, cols] = (r[CHUNK:] + _dot(qk, _block_diag(v_new, PAIR))).astype(o_ref.dtype)
        update = _dot_tn(kd_ref[i, :, cols], v_new)
        idx = slot(i, d, hp)
        state_ref[idx] = state_ref[idx] * egl_ref[i, :, cols] + jnp.where(on_diag, update, 0.0)


def _delta_scan(w, u, qg, kd, qk, egl):
    b, _, s, kw = w.shape
    nc = s // CHUNK
    chunk_of = (lambda c: c, lambda c: nc - 1 - c)
    in_specs = []
    for d in range(2):
        act = lambda n, d=d: pl.BlockSpec((b, None, CHUNK, n), lambda c: (0, d, chunk_of[d](c), 0))
        in_specs += [act(kw), act(kw), act(kw), act(kw), act(B_HEADS * CHUNK),
                     pl.BlockSpec((b, None, None, 1, kw), lambda c, d=d: (0, d, chunk_of[d](c), 0, 0))]
    out = jax.ShapeDtypeStruct((b, s, kw), BF16)
    out_specs = [pl.BlockSpec((b, CHUNK, kw), lambda c, d=d: (0, chunk_of[d](c), 0)) for d in range(2)]
    args = (w, u, qg, kd, qk, egl)
    return pl.pallas_call(
        _delta_scan_kernel,
        grid=(nc,),
        in_specs=in_specs,
        out_specs=out_specs,
        out_shape=[out, out],
        scratch_shapes=[pltpu.VMEM((b * 2 * N_PAIRS, PAIR * B_HEAD_K, PAIR * B_HEAD_V), F32)],
        compiler_params=_params(("arbitrary",)),
        name="delta_scan",
    )(*args, *args)


def _delta_out_kernel(of_ref, ob_ref, gate_ref, x_ref, onorm_ref, w_ref, fnorm_ref, o_ref, og_scr):
    onorm = onorm_ref[...]
    for h in range(B_HEADS):
        cols = slice(h * 128, (h + 1) * 128)
        o = of_ref[:, cols].astype(F32) + ob_ref[:, cols].astype(F32)
        o = o * lax.rsqrt(jnp.mean(o * o, axis=-1, keepdims=True) + EPS) * onorm
        gate = gate_ref[:, cols].astype(F32)
        og_scr[:, cols] = (o * (gate * _sigmoid(gate))).astype(BF16)
    y = x_ref[...] + _dot(og_scr[...], w_ref[...])
    o_ref[...] = _rms_rows(y, fnorm_ref[...])


def _delta_out(o_f, o_b, gate, x1, out_norm, w_out, final_norm):
    t = x1.shape[0]
    tm = 512
    assert t % tm == 0
    row = pl.BlockSpec((tm, D_MODEL), lambda i: (i, 0))
    vec = lambda n: pl.BlockSpec((1, n), lambda i: (0, 0))
    return pl.pallas_call(
        _delta_out_kernel,
        grid=(t // tm,),
        in_specs=[row, row, row, row, vec(B_HEAD_V), pl.BlockSpec(w_out.shape, lambda i: (0, 0)), vec(D_MODEL)],
        out_specs=row,
        out_shape=jax.ShapeDtypeStruct((t, D_MODEL), F32),
        scratch_shapes=[pltpu.VMEM((tm, D_MODEL), BF16)],
        compiler_params=_params(("parallel",)),
        name="delta_out",
    )(o_f, o_b, gate, x1, out_norm, w_out, final_norm)


def _attn_weights(w_in):
    kvw = A_KV_HEADS * A_HEAD_DIM
    wq = w_in[:, :D_MODEL]
    wk = w_in[:, D_MODEL:D_MODEL + kvw].reshape(D_MODEL, A_KV_HEADS, 1, A_HEAD_DIM)
    wv = w_in[:, D_MODEL + kvw:D_MODEL + 2 * kvw].reshape(D_MODEL, A_KV_HEADS, 1, A_HEAD_DIM)
    rep = lambda w: jnp.broadcast_to(w, (D_MODEL, A_KV_HEADS, 2, A_HEAD_DIM)).reshape(D_MODEL, 2 * kvw)
    wg = w_in[:, D_MODEL + 2 * kvw:]
    return jnp.concatenate([wq, rep(wk), rep(wv), wg], axis=1).astype(BF16)


def kernel(x, attn_norm, attn_w_in, attn_sink, attn_w_out, delta_norm, delta_w_in, delta_conv, delta_a_log,
           delta_dt_bias, delta_out_norm, delta_w_out, final_norm):
    b, s, d = x.shape
    assert d == D_MODEL
    t = b * s
    q, kv, gate = _attn_proj(x.reshape(t, d), attn_norm.reshape(1, d).astype(F32), _attn_weights(attn_w_in))
    shape3 = (b, s, d)
    x1 = _attn_core(x, q.reshape(shape3), kv.reshape(shape3), gate.reshape(shape3), attn_sink.astype(F32),
                    attn_w_out.astype(BF16))
    dq, dk, dv, dgate, col, row = _delta_proj(x1, delta_norm.reshape(1, d).astype(F32), delta_w_in, delta_conv,
                                              delta_a_log, delta_dt_bias)
    o_f, o_b = _delta_scan(*_delta_prep(dq, dk, dv, col, row))
    out = _delta_out(o_f.reshape(t, d), o_b.reshape(t, d), dgate.reshape(t, d), x1.reshape(t, d),
                     delta_out_norm.reshape(1, B_HEAD_V).astype(F32), delta_w_out.astype(BF16),
                     final_norm.reshape(1, d).astype(F32))
    return out.reshape(b, s, d)
```

```python
import functools
import itertools

import numpy as np
import jax
import jax.numpy as jnp
from jax import lax
from jax.experimental import pallas as pl
from jax.experimental.pallas import tpu as pltpu

F32 = jnp.float32
BF16 = jnp.bfloat16
EPS = 1e-6
LOG2E = float(np.log2(np.e))

D_MODEL = 1024
A_HEADS = 16
A_KV_HEADS = 4
A_HEAD_DIM = 64
A_PAIRS = A_HEADS // 2
BLOCK = 128
KEYS = 3 * BLOCK

B_HEADS = 8
B_HEAD_K = 128
B_HEAD_V = 128
CONV_WIDTH = 5
CONV_PAD = CONV_WIDTH // 2
CHUNK = 64
N_QKV = 3 * B_HEADS * B_HEAD_K
HALO = 8
PROJ_BLOCK = 256
PROJ_RING = 3

V7X_VMEM_LIMIT_BYTES = 56 * 1024 * 1024


def _sigmoid(x):
    return 1.0 / (1.0 + jnp.exp(-x))


def _softplus(x):
    return jnp.maximum(x, 0.0) + jnp.log(1.0 + jnp.exp(-jnp.abs(x)))


def _rms_rows(x, gain):
    ms = jnp.mean(x * x, axis=-1, keepdims=True)
    return x * lax.rsqrt(ms + EPS) * gain


def _split3(x):
    a = x.astype(BF16)
    r = x - a.astype(F32)
    b = r.astype(BF16)
    c = (r - b.astype(F32)).astype(BF16)
    return a, b, c


def _dot(a, b):
    return jnp.dot(a, b, preferred_element_type=F32)


def _dot_nt(a, b):
    return lax.dot_general(a, b, (((1,), (1,)), ((), ())), preferred_element_type=F32)


def _dot_tn(a, b):
    return lax.dot_general(a, b, (((0,), (0,)), ((), ())), preferred_element_type=F32)


def _params(semantics):
    return pltpu.CompilerParams(dimension_semantics=semantics, vmem_limit_bytes=V7X_VMEM_LIMIT_BYTES)


def _attn_proj_kernel(x_ref, gain_ref, w_ref, q_ref, kv_ref, gate_ref):
    hn = _rms_rows(x_ref[...], gain_ref[...]).astype(BF16)
    proj = _dot(hn, w_ref[...])
    q_ref[...] = (proj[:, :D_MODEL] * (A_HEAD_DIM ** -0.5 * LOG2E)).astype(BF16)
    kv_ref[...] = proj[:, D_MODEL:2 * D_MODEL].astype(BF16)
    gate_ref[...] = proj[:, 2 * D_MODEL:].astype(BF16)


def _attn_proj(x2, gain, w):
    t = x2.shape[0]
    tm = 512
    assert t % tm == 0
    out = jax.ShapeDtypeStruct((t, D_MODEL), BF16)
    row = pl.BlockSpec((tm, D_MODEL), lambda i: (i, 0))
    return pl.pallas_call(
        _attn_proj_kernel,
        grid=(t // tm,),
        in_specs=[row, pl.BlockSpec((1, D_MODEL), lambda i: (0, 0)), pl.BlockSpec(w.shape, lambda i: (0, 0))],
        out_specs=[row, row, row],
        out_shape=[out, out, out],
        compiler_params=_params(("parallel",)),
        name="attn_proj",
    )(x2, gain, w)


def _attn_core_kernel(sink_ref, q_ref, kvp_ref, kvc_ref, kvn_ref, gate_ref, x_ref, bias_ref, w_ref,
                      o_ref, kv_scr, *, tq, n_tiles):
    j = pl.program_id(1)
    kv_scr[0:BLOCK, :] = kvp_ref[...]
    kv_scr[BLOCK:BLOCK + tq, :] = kvc_ref[...]
    kv_scr[BLOCK + tq:, :] = kvn_ref[...]
    low = lax.broadcasted_iota(jnp.int32, (KEYS, 2 * A_HEAD_DIM), 1) < A_HEAD_DIM
    low_q = lax.broadcasted_iota(jnp.int32, (BLOCK, 2 * A_HEAD_DIM), 1) < A_HEAD_DIM
    zero = jnp.zeros((KEYS, 2 * A_HEAD_DIM), BF16)
    first_head_rows = lax.broadcasted_iota(jnp.int32, (2 * KEYS, 2 * A_HEAD_DIM), 0) < KEYS
    first_head_lanes = lax.broadcasted_iota(jnp.int32, (2 * KEYS, 2 * A_HEAD_DIM), 1) < A_HEAD_DIM
    ones_bd = jnp.where(first_head_rows == first_head_lanes, 1.0, 0.0).astype(BF16)
    n_qb = tq // BLOCK

    k_bd, v_ext = {}, {}

    def scores(qb, pair):
        rows = slice(qb * BLOCK, qb * BLOCK + KEYS)
        g = pair // 2
        if (qb, g) not in k_bd:
            k_rep = kv_scr[rows, g * 128:(g + 1) * 128]
            v_rep = kv_scr[rows, 512 + g * 128:512 + (g + 1) * 128]
            k_bd[qb, g] = jnp.concatenate([jnp.where(low, k_rep, zero), jnp.where(low, zero, k_rep)], axis=0)
            v_bd = jnp.concatenate([jnp.where(low, v_rep, zero), jnp.where(low, zero, v_rep)], axis=0)
            v_ext[qb, g] = jnp.concatenate([v_bd, ones_bd], axis=1)
        gqb = j * n_qb + qb
        table = jnp.where(gqb == 0, 1, jnp.where(gqb == n_tiles * n_qb - 1, 2, 0)) * A_PAIRS
        q_pair = q_ref[qb * BLOCK:(qb + 1) * BLOCK, pair * 128:(pair + 1) * 128]
        return _dot_nt(q_pair, k_bd[qb, g]) + bias_ref[table + pair]

    def out_chunk(qb, og_block, c):
        rows = slice(qb * BLOCK, (qb + 1) * BLOCK)
        cols = slice(c * 256, (c + 1) * 256)
        o_ref[rows, cols] = x_ref[rows, cols] + _dot(og_block, w_ref[:, cols])

    items = [(qb, pair) for qb in range(n_qb) for pair in range(A_PAIRS)]
    pending = []
    og_parts = []
    s_next = scores(*items[0])
    for n, (qb, pair) in enumerate(items):
        rows = slice(qb * BLOCK, (qb + 1) * BLOCK)
        s = s_next
        if n + 1 < len(items):
            s_next = scores(*items[n + 1])
        if pending and pair % 2 == 1:
            pending.pop(0)()
        probs, sink_mass = [], []
        for hh in range(2):
            sink = sink_ref[2 * pair + hh]
            sh = s[:, hh * KEYS:(hh + 1) * KEYS]
            m = jnp.maximum(jnp.max(sh, axis=-1, keepdims=True), sink)
            probs.append(jnp.exp2(sh - m).astype(BF16))
            sink_mass.append(jnp.exp2(sink - m))
        o_ext = _dot(jnp.concatenate(probs, axis=1), v_ext[qb, pair // 2])
        denom = o_ext[:, 128:] + jnp.where(low_q, sink_mass[0], sink_mass[1])
        gate = gate_ref[rows, pair * 128:(pair + 1) * 128].astype(F32)
        og_parts.append((o_ext[:, :128] * (1.0 / denom) * (gate * _sigmoid(gate))).astype(BF16))
        if pair == A_PAIRS - 1:
            og_block = jnp.concatenate(og_parts, axis=1)
            og_parts = []
            pending += [functools.partial(out_chunk, qb, og_block, c) for c in range(D_MODEL // 256)]
    for piece in pending:
        piece()


def _attn_tables():
    qpos = np.arange(BLOCK)[:, None]
    krel = np.arange(KEYS)[None, :] - BLOCK
    dist = np.abs(krel - qpos).astype(np.float32)
    slopes = np.power(2.0, -8.0 * np.arange(1, A_HEADS + 1) / A_HEADS).astype(np.float32)
    band = np.where(dist <= BLOCK, 0.0, -np.inf)
    per_head = (-slopes[:, None, None] * dist[None]).astype(np.float64) * LOG2E + band[None]
    sets = np.broadcast_to(per_head[None], (3, A_HEADS, BLOCK, KEYS)).copy()
    sets[1, :, :, :BLOCK] = -np.inf
    sets[2, :, :, 2 * BLOCK:] = -np.inf
    sets = sets.reshape(3, A_PAIRS, 2, BLOCK, KEYS).transpose(0, 1, 3, 2, 4)
    return jnp.asarray(sets.reshape(3 * A_PAIRS, BLOCK, 2 * KEYS), F32)


def _attn_core(x, q, kv, gate, sink, w_out):
    b, s, _ = x.shape
    tq = 512
    assert s % tq == 0 and s // BLOCK >= 2
    n_tiles = s // tq
    nb = s // BLOCK
    per = tq // BLOCK
    bias = _attn_tables()
    tile = pl.BlockSpec((None, tq, D_MODEL), lambda i, j: (i, j, 0))
    kernel = functools.partial(_attn_core_kernel, tq=tq, n_tiles=n_tiles)
    return pl.pallas_call(
        kernel,
        grid=(b, n_tiles),
        in_specs=[
            pl.BlockSpec(memory_space=pltpu.SMEM),
            tile,
            pl.BlockSpec((None, BLOCK, D_MODEL), lambda i, j: (i, jnp.maximum(j * per - 1, 0), 0)),
            tile,
            pl.BlockSpec((None, BLOCK, D_MODEL), lambda i, j: (i, jnp.minimum((j + 1) * per, nb - 1), 0)),
            tile,
            tile,
            pl.BlockSpec(bias.shape, lambda i, j: (0, 0, 0), pipeline_mode=pl.Buffered(1)),
            pl.BlockSpec(w_out.shape, lambda i, j: (0, 0), pipeline_mode=pl.Buffered(1)),
        ],
        out_specs=tile,
        out_shape=jax.ShapeDtypeStruct(x.shape, F32),
        scratch_shapes=[pltpu.VMEM((tq + 2 * BLOCK, D_MODEL), BF16)],
        compiler_params=_params(("parallel", "parallel")),
        name="attn_core",
    )(sink * LOG2E, q, kv, kv, kv, gate, x, bias, w_out)


def _proj_stages(j, xp_ref, xc_ref, xn_ref, gain_ref, wqkv_ref, wg_ref, wba_ref, wbat_ref, conv_ref,
                 alog_ref, dtb_ref, alogc_ref, dtbc_ref, tril_ref, triu_ref, trilt_ref, triut_ref,
                 q_ref, k_ref, v_ref, gate_ref, col_ref, row_ref, hn_scr, proj_scr, *, tm, n_tiles):
    gain = gain_ref[...]
    keep_prev = (j > 0).astype(F32)
    keep_next = (j < n_tiles - 1).astype(F32)
    hn_scr[0:HALO, :] = (_rms_rows(xp_ref[...], gain) * keep_prev).astype(BF16)
    hn_scr[HALO:HALO + tm, :] = _rms_rows(xc_ref[...], gain).astype(BF16)
    hn_scr[HALO + tm:, :] = (_rms_rows(xn_ref[...], gain) * keep_next).astype(BF16)
    hn = hn_scr[HALO:HALO + tm, :]
    gate_ref[...] = _dot(hn, wg_ref[...]).astype(BF16)

    ba = _dot(hn, wba_ref[...])
    g_c = -jnp.exp(alog_ref[...]) * _softplus(ba + dtb_ref[...])
    bat = _dot_nt(wbat_ref[...], hn)
    g_r = -jnp.exp(alogc_ref[...]) * _softplus(bat + dtbc_ref[...])
    gc3 = _split3(g_c)
    gr3 = _split3(g_r)
    ent_r = lax.broadcasted_iota(jnp.int32, (4 * B_HEADS, tm), 0)
    cum_r = jnp.where(ent_r < 3 * B_HEADS, sum(_dot(p, trilt_ref[...]) for p in gr3),
                      sum(_dot(p, triut_ref[...]) for p in gr3))
    row_ref[...] = jnp.where(ent_r < 2 * B_HEADS, _sigmoid(bat), cum_r)
    ent_c = lax.broadcasted_iota(jnp.int32, (CHUNK, 4 * B_HEADS), 1)
    beta_c = _sigmoid(ba)
    for c in range(tm // CHUNK):
        rows = slice(c * CHUNK, (c + 1) * CHUNK)
        fwd = sum(_dot(tril_ref[...], p[rows]) for p in gc3)
        bwd = sum(_dot(triu_ref[...], p[rows]) for p in gc3)
        col_ref[rows, :] = jnp.where(ent_c < 2 * B_HEADS, beta_c[rows], jnp.where(ent_c < 3 * B_HEADS, fwd, bwd))

    taps = conv_ref[...]
    hn_ext = hn_scr[...]
    n_blocks = N_QKV // PROJ_BLOCK
    rc = 128

    def project(n):
        proj_scr[n % PROJ_RING] = _dot(hn_ext, wqkv_ref[:, n * PROJ_BLOCK:(n + 1) * PROJ_BLOCK])

    def conv(n):
        for half in range(PROJ_BLOCK // 128):
            cb = n * (PROJ_BLOCK // 128) + half
            cols = slice(cb * 128, (cb + 1) * 128)
            out_ref, head = ((q_ref, k_ref, v_ref)[cb // B_HEADS], cb % B_HEADS)
            for r0 in range(0, tm, rc):
                window = proj_scr[n % PROJ_RING, r0:r0 + rc + 2 * HALO, half * 128:(half + 1) * 128]
                acc = None
                for t in range(CONV_WIDTH):
                    shift = CONV_PAD - t
                    moved = window if shift == 0 else pltpu.roll(window, shift % (rc + 2 * HALO), axis=0)
                    term = taps[t:t + 1, cols] * moved[HALO:HALO + rc]
                    acc = term if acc is None else acc + term
                y = acc * _sigmoid(acc)
                if out_ref is not v_ref:
                    scale = 1.0 if out_ref is k_ref else B_HEAD_K ** -0.5
                    y = y * (lax.rsqrt(jnp.sum(y * y, axis=-1, keepdims=True) + EPS) * scale)
                out_ref[r0:r0 + rc, head * 128:(head + 1) * 128] = y.astype(BF16)

    project(0)
    yield
    for n in range(n_blocks):
        if n + 1 < n_blocks:
            project(n + 1)
        yield
        conv(n)


def _delta_front_kernel(*refs, tm, n_tiles, n_total):
    n_in, n_out = 17, 7
    ins, outs, scr = refs[:n_in], refs[n_in:n_in + n_out], refs[n_in + n_out:]
    gate_ref, w_ref, u_ref, qg_ref, kd_ref, qk_ref, egl_ref = outs
    hn_scr, proj_scr, q_scr, k_scr, v_scr, col_scr, row_scr = scr
    j = pl.program_id(0)

    @pl.when(j == 0)
    def _():
        for ref in (q_scr, k_scr, v_scr, col_scr, row_scr):
            ref[1] = jnp.zeros(ref.shape[1:], ref.dtype)

    cur = j % 2
    prev = 1 - cur
    tile = lax.rem(jnp.minimum(j, n_total - 1), n_tiles)
    proj = _proj_stages(tile, *ins, q_scr.at[cur], k_scr.at[cur], v_scr.at[cur], gate_ref, col_scr.at[cur],
                        row_scr.at[cur], hn_scr, proj_scr, tm=tm, n_tiles=n_tiles)
    n_chunks = tm // CHUNK
    waves = [range(c0, c0 + PREP_WAVE) for c0 in range(0, n_chunks, PREP_WAVE)]
    prep = itertools.chain.from_iterable(
        _prep_stages(q_scr.at[prev], k_scr.at[prev], v_scr.at[prev], col_scr.at[prev], row_scr.at[prev],
                     w_ref, u_ref, qg_ref, kd_ref, qk_ref, egl_ref, chunk_ids=wave) for wave in waves)
    live = {"proj": proj, "prep": prep}
    turn = 0
    while live:
        for name in ("proj", "prep", "prep") if turn % 2 else ("proj", "prep"):
            if name in live and next(live[name], StopIteration) is StopIteration:
                del live[name]
        turn += 1


def _delta_front(x1, gain, w_in, conv_w, a_log, dt_bias):
    b, s, _ = x1.shape
    tm = FRONT_TILE
    assert s % tm == 0
    n_tiles = s // tm
    nc = s // CHUNK
    per = tm // HALO
    nh = s // HALO
    kw = B_HEADS * B_HEAD_K
    wqkv = w_in[:, :N_QKV].astype(BF16)
    wg = w_in[:, N_QKV:N_QKV + kw].astype(BF16)
    wba = w_in[:, N_QKV + kw:].astype(BF16)
    wbat = wba.T
    pad = jnp.zeros((1, 2 * B_HEADS), F32)
    alog = jnp.concatenate([pad, a_log.reshape(1, 2 * B_HEADS).astype(F32)], axis=1)
    dtb = jnp.concatenate([pad, dt_bias.reshape(1, 2 * B_HEADS).astype(F32)], axis=1)
    i = np.arange(CHUNK)
    tril = (i[:, None] >= i[None, :]).astype(np.float32)
    triu = tril.T
    eye = np.eye(tm // CHUNK, dtype=np.float32)
    trilt = jnp.asarray(np.kron(eye, tril.T), BF16)
    triut = jnp.asarray(np.kron(eye, triu.T), BF16)
    tril = jnp.asarray(tril, BF16)
    triu = jnp.asarray(triu, BF16)
    full = lambda a: pl.BlockSpec(a.shape, lambda j_: (0,) * a.ndim, pipeline_mode=pl.Buffered(1))
    n_total = b * n_tiles
    cur = lambda j_: divmod(jnp.minimum(j_, n_total - 1), n_tiles)
    done = lambda j_: divmod(jnp.maximum(j_ - 1, 0), n_tiles)
    both = lambda n: pl.BlockSpec((None, 2, tm, n), lambda j_: (done(j_)[0], 0, done(j_)[1], 0))
    act = jax.ShapeDtypeStruct((b, 2, s, kw), BF16)
    kernel = functools.partial(_delta_front_kernel, tm=tm, n_tiles=n_tiles, n_total=n_total)
    consts = [gain, wqkv, wg, wba, wbat, conv_w.astype(F32), alog, dtb, alog.T, dtb.T, tril, triu, trilt, triut]
    return pl.pallas_call(
        kernel,
        grid=(n_total + 1,),
        in_specs=[
            pl.BlockSpec((None, HALO, D_MODEL), lambda j_: (cur(j_)[0], jnp.maximum(cur(j_)[1] * per - 1, 0), 0)),
            pl.BlockSpec((None, tm, D_MODEL), lambda j_: (cur(j_)[0], cur(j_)[1], 0)),
            pl.BlockSpec((None, HALO, D_MODEL),
                         lambda j_: (cur(j_)[0], jnp.minimum((cur(j_)[1] + 1) * per, nh - 1), 0)),
        ] + [full(a) for a in consts],
        out_specs=[pl.BlockSpec((None, tm, kw), lambda j_: (cur(j_)[0], cur(j_)[1], 0)),
                   both(kw), both(kw), both(kw), both(kw), both(B_HEADS * CHUNK),
                   pl.BlockSpec((None, 2, tm // CHUNK, 1, kw), lambda j_: (done(j_)[0], 0, done(j_)[1], 0, 0))],
        out_shape=[jax.ShapeDtypeStruct((b, s, kw), BF16), act, act, act, act,
                   jax.ShapeDtypeStruct((b, 2, s, B_HEADS * CHUNK), BF16),
                   jax.ShapeDtypeStruct((b, 2, nc, 1, kw), F32)],
        scratch_shapes=[pltpu.VMEM((tm + 2 * HALO, D_MODEL), BF16),
                        pltpu.VMEM((PROJ_RING, tm + 2 * HALO, PROJ_BLOCK), F32),
                        pltpu.VMEM((2, tm, kw), BF16), pltpu.VMEM((2, tm, kw), BF16), pltpu.VMEM((2, tm, kw), BF16),
                        pltpu.VMEM((2, tm, 4 * B_HEADS), F32), pltpu.VMEM((2, 4 * B_HEADS, tm), F32)],
        compiler_params=_params(("arbitrary",)),
        name="delta_front",
    )(x1, x1, x1, *consts)


FRONT_TILE = 512
PREP_WAVE = 4
GROUP = 4
N_GROUPS = B_HEADS // GROUP
PAIR = 2
N_PAIRS = B_HEADS // PAIR


def _block_diag(x, n_blocks):
    r, width = x.shape
    shift = (width // n_blocks).bit_length() - 1
    blk = lax.broadcasted_iota(jnp.int32, (r, width), 1) >> shift
    zero = jnp.zeros_like(x)
    return jnp.concatenate([jnp.where(blk == a, x, zero) for a in range(n_blocks)], axis=0)


def _prep_stages(q_ref, k_ref, v_ref, col_ref, row_ref, w_ref, u_ref, qg_ref, kd_ref, qk_ref, egl_ref, *, chunk_ids):
    width = GROUP * CHUNK

    def row_piece(kind, c, g):
        first = kind * B_HEADS + g * GROUP
        return jnp.concatenate([row_ref[first + a:first + a + 1, c * CHUNK:(c + 1) * CHUNK] for a in range(GROUP)],
                               axis=1)

    ri = lax.broadcasted_iota(jnp.int32, (CHUNK, width), 0)
    cj = lax.broadcasted_iota(jnp.int32, (CHUNK, width), 1) & (CHUNK - 1)
    low_half = lax.broadcasted_iota(jnp.int32, (CHUNK, 128), 1) < CHUNK
    eye = (ri == cj).astype(F32)
    masks = ((ri >= cj, ri > cj, CHUNK - 1), (ri <= cj, ri < cj, 0))

    def side_by_side(parts):
        return jnp.concatenate([jnp.where(low_half, parts[0], parts[1]), jnp.where(low_half, parts[2], parts[3])],
                               axis=1)

    gram, kbd = {}, {}
    for c in chunk_ids:
        rows = slice(c * CHUNK, (c + 1) * CHUNK)
        for g in range(N_GROUPS):
            cols = slice(g * 512, (g + 1) * 512)
            k4 = k_ref[rows, cols]
            kbd[c, g] = _block_diag(k4, GROUP)
            gram[c, g] = _dot_nt(jnp.concatenate([q_ref[rows, cols], k4], axis=0), kbd[c, g])
    yield

    x, p = {}, {}
    for c in chunk_ids:
        rows = slice(c * CHUNK, (c + 1) * CHUNK)
        for d, (incl, strict, last) in enumerate(masks):
            for g in range(N_GROUPS):
                beta_cols, gc_cols = [], []
                for a in range(GROUP):
                    h = g * GROUP + a
                    idx = d * B_HEADS + h
                    hcols = slice(h * 128, (h + 1) * 128)
                    beta_b = jnp.broadcast_to(col_ref[rows, idx:idx + 1], (CHUNK, 128))
                    gc_b = jnp.broadcast_to(col_ref[rows, 2 * B_HEADS + idx:2 * B_HEADS + idx + 1], (CHUNK, 128))
                    t_last = c * CHUNK + last
                    g_last = row_ref[2 * B_HEADS + idx:2 * B_HEADS + idx + 1, t_last:t_last + 1]
                    qg_ref[d, rows, hcols] = (q_ref[rows, hcols].astype(F32) * jnp.exp(gc_b)).astype(BF16)
                    kd_ref[d, rows, hcols] = (k_ref[rows, hcols].astype(F32) * jnp.exp(g_last - gc_b)).astype(BF16)
                    egl_ref[d, c, :, hcols] = jnp.exp(jnp.broadcast_to(g_last, (1, 128)))
                    beta_cols.append(beta_b)
                    gc_cols.append(gc_b)
                lanes = slice(g * width, (g + 1) * width)
                gc_row = row_piece(2 + d, c, g)
                decay = jnp.where(incl, jnp.exp(jnp.where(incl, side_by_side(gc_cols) - gc_row, 0.0)), 0.0)
                gm = gram[c, g]
                qk_ref[d, rows, lanes] = (gm[:CHUNK] * decay).astype(BF16)
                low = jnp.where(strict, gm[CHUNK:] * decay * side_by_side(beta_cols), 0.0)
                x[c, d, g] = -low
                p[c, d, g] = eye - low
        if c % 2 == 1:
            yield

    keys = list(x)
    for key in keys:
        xb = x[key].astype(BF16)
        x[key] = _dot(xb, _block_diag(xb, GROUP))
    yield
    squarings = CHUNK.bit_length() - 2
    for _ in range(squarings - 1):
        for key in keys:
            xb = x[key].astype(BF16)
            z = _dot(jnp.concatenate([xb, p[key].astype(BF16)], axis=0), _block_diag(xb, GROUP))
            x[key] = z[:CHUNK]
            p[key] = p[key] + z[CHUNK:]
        yield
    for key in keys:
        p[key] = p[key] + _dot(p[key].astype(BF16), _block_diag(x[key].astype(BF16), GROUP))
    yield

    for c in chunk_ids:
        rows = slice(c * CHUNK, (c + 1) * CHUNK)
        for g in range(N_GROUPS):
            cols = slice(g * 512, (g + 1) * 512)
            tb = [p[c, d, g] * row_piece(d, c, g) for d in range(2)]
            tbg = [tb[d] * jnp.exp(row_piece(2 + d, c, g)) for d in range(2)]
            uu = _dot(jnp.concatenate(tb, axis=0).astype(BF16), _block_diag(v_ref[rows, cols], GROUP))
            ww = _dot(jnp.concatenate(tbg, axis=0).astype(BF16), kbd[c, g])
            for d in range(2):
                u_ref[d, rows, cols] = uu[d * CHUNK:(d + 1) * CHUNK].astype(BF16)
                w_ref[d, rows, cols] = ww[d * CHUNK:(d + 1) * CHUNK].astype(BF16)
    yield


def _delta_scan_kernel(*refs):
    per_dir = 6
    dirs = (refs[:per_dir] + (refs[2 * per_dir],), refs[per_dir:2 * per_dir] + (refs[2 * per_dir + 1],))
    state_ref = refs[2 * per_dir + 2]
    n_batch = dirs[0][0].shape[0]

    @pl.when(pl.program_id(0) == 0)
    def _():
        state_ref[...] = jnp.zeros_like(state_ref)

    tile = PAIR * B_HEAD_K
    on_diag = ((lax.broadcasted_iota(jnp.int32, (tile, tile), 0) >> 7)
               == (lax.broadcasted_iota(jnp.int32, (tile, tile), 1) >> 7))
    units = [(i, d, hp) for i in range(n_batch) for d in range(2) for hp in range(N_PAIRS)]
    slot = lambda i, d, hp: (i * 2 + d) * N_PAIRS + hp
    rs = {}
    for i, d, hp in units:
        w_ref, _, qg_ref = dirs[d][:3]
        cols = slice(hp * tile, (hp + 1) * tile)
        lhs = jnp.concatenate([w_ref[i, :, cols], qg_ref[i, :, cols]], axis=0)
        rs[i, d, hp] = _dot(lhs, state_ref[slot(i, d, hp)].astype(BF16))
    for i, d, hp in units:
        _, u_ref, _, kd_ref, qk_ref, egl_ref, o_ref = dirs[d]
        cols = slice(hp * tile, (hp + 1) * tile)
        r = rs[i, d, hp]
        v_new = (u_ref[i, :, cols].astype(F32) - r[:CHUNK]).astype(BF16)
        qk = qk_ref[i, :, hp * PAIR * CHUNK:(hp + 1) * PAIR * CHUNK]
        o_ref[i, :, cols] = (r[CHUNK:] + _dot(qk, _block_diag(v_new, PAIR))).astype(o_ref.dtype)
        update = _dot_tn(kd_ref[i, :, cols], v_new)
        idx = slot(i, d, hp)
        state_ref[idx] = state_ref[idx] * egl_ref[i, :, cols] + jnp.where(on_diag, update, 0.0)


def _delta_scan(w, u, qg, kd, qk, egl):
    b, _, s, kw = w.shape
    nc = s // CHUNK
    chunk_of = (lambda c: c, lambda c: nc - 1 - c)
    in_specs = []
    for d in range(2):
        act = lambda n, d=d: pl.BlockSpec((b, None, CHUNK, n), lambda c: (0, d, chunk_of[d](c), 0))
        in_specs += [act(kw), act(kw), act(kw), act(kw), act(B_HEADS * CHUNK),
                     pl.BlockSpec((b, None, None, 1, kw), lambda c, d=d: (0, d, chunk_of[d](c), 0, 0))]
    out = jax.ShapeDtypeStruct((b, s, kw), BF16)
    out_specs = [pl.BlockSpec((b, CHUNK, kw), lambda c, d=d: (0, chunk_of[d](c), 0)) for d in range(2)]
    args = (w, u, qg, kd, qk, egl)
    return pl.pallas_call(
        _delta_scan_kernel,
        grid=(nc,),
        in_specs=in_specs,
        out_specs=out_specs,
        out_shape=[out, out],
        scratch_shapes=[pltpu.VMEM((b * 2 * N_PAIRS, PAIR * B_HEAD_K, PAIR * B_HEAD_V), F32)],
        compiler_params=_params(("arbitrary",)),
        name="delta_scan",
    )(*args, *args)


def _delta_out_kernel(of_ref, ob_ref, gate_ref, x_ref, onorm_ref, w_ref, fnorm_ref, o_ref, og_scr):
    onorm = onorm_ref[...]
    for h in range(B_HEADS):
        cols = slice(h * 128, (h + 1) * 128)
        o = of_ref[:, cols].astype(F32) + ob_ref[:, cols].astype(F32)
        o = o * lax.rsqrt(jnp.mean(o * o, axis=-1, keepdims=True) + EPS) * onorm
        gate = gate_ref[:, cols].astype(F32)
        og_scr[:, cols] = (o * (gate * _sigmoid(gate))).astype(BF16)
    y = x_ref[...] + _dot(og_scr[...], w_ref[...])
    o_ref[...] = _rms_rows(y, fnorm_ref[...])


def _delta_out(o_f, o_b, gate, x1, out_norm, w_out, final_norm):
    t = x1.shape[0]
    tm = 512
    assert t % tm == 0
    row = pl.BlockSpec((tm, D_MODEL), lambda i: (i, 0))
    vec = lambda n: pl.BlockSpec((1, n), lambda i: (0, 0))
    return pl.pallas_call(
        _delta_out_kernel,
        grid=(t // tm,),
        in_specs=[row, row, row, row, vec(B_HEAD_V), pl.BlockSpec(w_out.shape, lambda i: (0, 0)), vec(D_MODEL)],
        out_specs=row,
        out_shape=jax.ShapeDtypeStruct((t, D_MODEL), F32),
        scratch_shapes=[pltpu.VMEM((tm, D_MODEL), BF16)],
        compiler_params=_params(("parallel",)),
        name="delta_out",
    )(o_f, o_b, gate, x1, out_norm, w_out, final_norm)


def _attn_weights(w_in):
    kvw = A_KV_HEADS * A_HEAD_DIM
    wq = w_in[:, :D_MODEL]
    wk = w_in[:, D_MODEL:D_MODEL + kvw].reshape(D_MODEL, A_KV_HEADS, 1, A_HEAD_DIM)
    wv = w_in[:, D_MODEL + kvw:D_MODEL + 2 * kvw].reshape(D_MODEL, A_KV_HEADS, 1, A_HEAD_DIM)
    rep = lambda w: jnp.broadcast_to(w, (D_MODEL, A_KV_HEADS, 2, A_HEAD_DIM)).reshape(D_MODEL, 2 * kvw)
    wg = w_in[:, D_MODEL + 2 * kvw:]
    return jnp.concatenate([wq, rep(wk), rep(wv), wg], axis=1).astype(BF16)


def kernel(x, attn_norm, attn_w_in, attn_sink, attn_w_out, delta_norm, delta_w_in, delta_conv, delta_a_log,
           delta_dt_bias, delta_out_norm, delta_w_out, final_norm):
    b, s, d = x.shape
    assert d == D_MODEL
    t = b * s
    q, kv, gate = _attn_proj(x.reshape(t, d), attn_norm.reshape(1, d).astype(F32), _attn_weights(attn_w_in))
    shape3 = (b, s, d)
    x1 = _attn_core(x, q.reshape(shape3), kv.reshape(shape3), gate.reshape(shape3), attn_sink.astype(F32),
                    attn_w_out.astype(BF16))
    dgate, *operators = _delta_front(x1, delta_norm.reshape(1, d).astype(F32), delta_w_in, delta_conv,
                                     delta_a_log, delta_dt_bias)
    o_f, o_b = _delta_scan(*operators)
    out = _delta_out(o_f.reshape(t, d), o_b.reshape(t, d), dgate.reshape(t, d), x1.reshape(t, d),
                     delta_out_norm.reshape(1, B_HEAD_V).astype(F32), delta_w_out.astype(BF16),
                     final_norm.reshape(1, d).astype(F32))
    return out.reshape(b, s, d)
```

```python
import functools
import itertools

import numpy as np
import jax
import jax.numpy as jnp
from jax import lax
from jax.experimental import pallas as pl
from jax.experimental.pallas import tpu as pltpu

F32 = jnp.float32
BF16 = jnp.bfloat16
EPS = 1e-6
LOG2E = float(np.log2(np.e))

D_MODEL = 1024
A_HEADS = 16
A_KV_HEADS = 4
A_HEAD_DIM = 64
A_PAIRS = A_HEADS // 2
BLOCK = 128
KEYS = 3 * BLOCK

B_HEADS = 8
B_HEAD_K = 128
B_HEAD_V = 128
CONV_WIDTH = 5
CONV_PAD = CONV_WIDTH // 2
CHUNK = 64
N_QKV = 3 * B_HEADS * B_HEAD_K
HALO = 8
PROJ_BLOCK = 256
PROJ_RING = 3

V7X_VMEM_LIMIT_BYTES = 56 * 1024 * 1024


def _sigmoid(x):
    return 1.0 / (1.0 + jnp.exp(-x))


def _silu(x):
    half = 0.5 * x
    return half + half * jnp.tanh(half)


def _softplus(x):
    return jnp.maximum(x, 0.0) + jnp.log(1.0 + jnp.exp(-jnp.abs(x)))


def _rms_rows(x, gain):
    ms = jnp.mean(x * x, axis=-1, keepdims=True)
    return x * lax.rsqrt(ms + EPS) * gain


def _split3(x):
    a = x.astype(BF16)
    r = x - a.astype(F32)
    b = r.astype(BF16)
    c = (r - b.astype(F32)).astype(BF16)
    return a, b, c


def _dot(a, b):
    return jnp.dot(a, b, preferred_element_type=F32)


def _dot_nt(a, b):
    return lax.dot_general(a, b, (((1,), (1,)), ((), ())), preferred_element_type=F32)


def _dot_tn(a, b):
    return lax.dot_general(a, b, (((0,), (0,)), ((), ())), preferred_element_type=F32)


def _params(semantics):
    return pltpu.CompilerParams(dimension_semantics=semantics, vmem_limit_bytes=V7X_VMEM_LIMIT_BYTES)


def _attn_proj_kernel(x_ref, gain_ref, wqt_ref, wk_ref, wvt_ref, wgt_ref, qt_ref, k_ref, vt_ref, gt_ref):
    hn = _rms_rows(x_ref[...], gain_ref[...]).astype(BF16)
    qt_ref[...] = (_dot_nt(wqt_ref[...], hn) * (A_HEAD_DIM ** -0.5 * LOG2E)).astype(BF16)
    k_ref[...] = _dot(hn, wk_ref[...]).astype(BF16)
    vt_ref[...] = _dot_nt(wvt_ref[...], hn).astype(BF16)
    gt_ref[...] = _dot_nt(wgt_ref[...], hn).astype(BF16)


def _attn_proj(x, gain, w_in):
    b, s, _ = x.shape
    tm = 512
    assert s % tm == 0
    kvw = A_KV_HEADS * A_HEAD_DIM
    wqt = w_in[:, :D_MODEL].T.astype(BF16)
    wk = w_in[:, D_MODEL:D_MODEL + kvw].reshape(D_MODEL, A_KV_HEADS, A_HEAD_DIM)
    wk = jnp.concatenate([wk, jnp.zeros_like(wk)], axis=2).reshape(D_MODEL, 2 * kvw).astype(BF16)
    wvt = w_in[:, D_MODEL + kvw:D_MODEL + 2 * kvw].T.astype(BF16)
    wgt = w_in[:, D_MODEL + 2 * kvw:].T.astype(BF16)
    full = lambda a: pl.BlockSpec(a.shape, lambda i, j: (0, 0), pipeline_mode=pl.Buffered(1))
    feat = lambda n: pl.BlockSpec((None, n, tm), lambda i, j: (i, 0, j))
    return pl.pallas_call(
        _attn_proj_kernel,
        grid=(b, s // tm),
        in_specs=[pl.BlockSpec((None, tm, D_MODEL), lambda i, j: (i, j, 0)), full(gain),
                  full(wqt), full(wk), full(wvt), full(wgt)],
        out_specs=[feat(D_MODEL), pl.BlockSpec((None, tm, 2 * kvw), lambda i, j: (i, j, 0)), feat(kvw), feat(D_MODEL)],
        out_shape=[jax.ShapeDtypeStruct((b, D_MODEL, s), BF16), jax.ShapeDtypeStruct((b, s, 2 * kvw), BF16),
                   jax.ShapeDtypeStruct((b, kvw, s), BF16), jax.ShapeDtypeStruct((b, D_MODEL, s), BF16)],
        compiler_params=_params(("parallel", "parallel")),
        name="attn_proj",
    )(x, gain, wqt, wk, wvt, wgt)


A_GROUP = A_HEADS // A_KV_HEADS
ONES_ROWS = 16


def _attn_core_kernel(sink_ref, qt_ref, kp_ref, kc_ref, kn_ref, vp_ref, vc_ref, vn_ref, gt_ref, x_ref, bias_ref,
                      w_ref, o_ref, k_scr, vt_scr, *, tq, n_tiles):
    j = pl.program_id(1)
    k_scr[0:BLOCK, :] = kp_ref[...]
    k_scr[BLOCK:BLOCK + tq, :] = kc_ref[...]
    k_scr[BLOCK + tq:, :] = kn_ref[...]
    vt_scr[:, 0:BLOCK] = vp_ref[...]
    vt_scr[:, BLOCK:BLOCK + tq] = vc_ref[...]
    vt_scr[:, BLOCK + tq:] = vn_ref[...]
    zero_rows = jnp.zeros((A_HEAD_DIM, BLOCK), BF16)
    ones_rows = jnp.ones((ONES_ROWS, KEYS), BF16)
    n_qb = tq // BLOCK

    def scores(qb, g):
        keys = slice(qb * BLOCK, qb * BLOCK + KEYS)
        qcols = slice(qb * BLOCK, (qb + 1) * BLOCK)
        q_t = jnp.concatenate(
            [jnp.concatenate([qt_ref[(A_GROUP * g + r) * A_HEAD_DIM:(A_GROUP * g + r + 1) * A_HEAD_DIM, qcols],
                              zero_rows], axis=0) for r in range(A_GROUP)], axis=1)
        gqb = j * n_qb + qb
        table = jnp.where(gqb == 0, 1, jnp.where(gqb == n_tiles * n_qb - 1, 2, 0)) * A_KV_HEADS
        return _dot(k_scr[keys, g * 128:(g + 1) * 128], q_t) + bias_ref[table + g]

    def out_chunk(qb, og_t, c):
        rows = slice(qb * BLOCK, (qb + 1) * BLOCK)
        cols = slice(c * 256, (c + 1) * 256)
        o_ref[rows, cols] = x_ref[rows, cols] + _dot_tn(og_t, w_ref[:, cols])

    items = [(qb, g) for qb in range(n_qb) for g in range(A_KV_HEADS)]
    pending = []
    og_parts = []
    s_next = scores(*items[0])
    for n, (qb, g) in enumerate(items):
        keys = slice(qb * BLOCK, qb * BLOCK + KEYS)
        qcols = slice(qb * BLOCK, (qb + 1) * BLOCK)
        s = s_next
        if n + 1 < len(items):
            s_next = scores(*items[n + 1])
        if pending:
            pending.pop(0)()
        sink = sink_ref[g]
        m = jnp.maximum(jnp.max(s, axis=0, keepdims=True), sink)
        probs = jnp.exp2(s - m).astype(BF16)
        v_t = jnp.concatenate([vt_scr[g * A_HEAD_DIM:(g + 1) * A_HEAD_DIM, keys], ones_rows], axis=0)
        o_ext = _dot(v_t, probs)
        inv = 1.0 / (o_ext[A_HEAD_DIM:A_HEAD_DIM + 1] + jnp.exp2(sink - m))
        for r in range(A_GROUP):
            h = A_GROUP * g + r
            lanes = slice(r * BLOCK, (r + 1) * BLOCK)
            gate = gt_ref[h * A_HEAD_DIM:(h + 1) * A_HEAD_DIM, qcols].astype(F32)
            og_parts.append((o_ext[:A_HEAD_DIM, lanes] * inv[:, lanes] * _silu(gate)).astype(BF16))
        if g == A_KV_HEADS - 1:
            og_t = jnp.concatenate(og_parts, axis=0)
            og_parts = []
            pending += [functools.partial(out_chunk, qb, og_t, c) for c in range(D_MODEL // 256)]
    for piece in pending:
        piece()


def _attn_tables():
    qpos = np.arange(BLOCK)[:, None]
    krel = np.arange(KEYS)[None, :] - BLOCK
    dist = np.abs(krel - qpos).astype(np.float32)
    slopes = np.power(2.0, -8.0 * np.arange(1, A_HEADS + 1) / A_HEADS).astype(np.float32)
    band = np.where(dist <= BLOCK, 0.0, -np.inf)
    per_head = (-slopes[:, None, None] * dist[None]).astype(np.float64) * LOG2E + band[None]
    sets = np.broadcast_to(per_head[None], (3, A_HEADS, BLOCK, KEYS)).copy()
    sets[1, :, :, :BLOCK] = -np.inf
    sets[2, :, :, 2 * BLOCK:] = -np.inf
    sets = sets.reshape(3, A_KV_HEADS, A_GROUP, BLOCK, KEYS).transpose(0, 1, 4, 2, 3)
    return jnp.asarray(sets.reshape(3 * A_KV_HEADS, KEYS, A_GROUP * BLOCK), F32)


def _attn_core(x, q_t, k, v_t, gate_t, sink, w_out):
    b, s, _ = x.shape
    tq = 512
    assert s % tq == 0 and s // BLOCK >= 2
    n_tiles = s // tq
    nb = s // BLOCK
    per = tq // BLOCK
    kw = k.shape[-1]
    vw = v_t.shape[1]
    bias = _attn_tables()
    sink_rows = jnp.repeat((sink * LOG2E).reshape(A_KV_HEADS, 1, A_GROUP), BLOCK, axis=2)
    tile = pl.BlockSpec((None, tq, D_MODEL), lambda i, j: (i, j, 0))
    feat = lambda n: pl.BlockSpec((None, n, tq), lambda i, j: (i, 0, j))
    prev = lambda j: jnp.maximum(j * per - 1, 0)
    nxt = lambda j: jnp.minimum((j + 1) * per, nb - 1)
    const = lambda a: pl.BlockSpec(a.shape, lambda i, j: (0,) * a.ndim, pipeline_mode=pl.Buffered(1))
    kernel = functools.partial(_attn_core_kernel, tq=tq, n_tiles=n_tiles)
    return pl.pallas_call(
        kernel,
        grid=(b, n_tiles),
        in_specs=[
            const(sink_rows),
            feat(D_MODEL),
            pl.BlockSpec((None, BLOCK, kw), lambda i, j: (i, prev(j), 0)),
            pl.BlockSpec((None, tq, kw), lambda i, j: (i, j, 0)),
            pl.BlockSpec((None, BLOCK, kw), lambda i, j: (i, nxt(j), 0)),
            pl.BlockSpec((None, vw, BLOCK), lambda i, j: (i, 0, prev(j))),
            feat(vw),
            pl.BlockSpec((None, vw, BLOCK), lambda i, j: (i, 0, nxt(j))),
            feat(D_MODEL),
            tile,
            const(bias),
            const(w_out),
        ],
        out_specs=tile,
        out_shape=jax.ShapeDtypeStruct(x.shape, F32),
        scratch_shapes=[pltpu.VMEM((tq + 2 * BLOCK, kw), BF16), pltpu.VMEM((vw, tq + 2 * BLOCK), BF16)],
        compiler_params=_params(("parallel", "parallel")),
        name="attn_core",
    )(sink_rows, q_t, k, k, k, v_t, v_t, v_t, gate_t, x, bias, w_out)


def _proj_stages(j, xp_ref, xc_ref, xn_ref, gain_ref, wqkv_ref, wg_ref, wba_ref, wbat_ref, conv_ref,
                 alog_ref, dtb_ref, alogc_ref, dtbc_ref, tril_ref, triu_ref, trilt_ref, triut_ref,
                 q_ref, k_ref, v_ref, gate_ref, col_ref, row_ref, hn_scr, proj_scr, *, tm, n_tiles):
    gain = gain_ref[...]
    keep_prev = (j > 0).astype(F32)
    keep_next = (j < n_tiles - 1).astype(F32)
    hn_scr[0:HALO, :] = (_rms_rows(xp_ref[...], gain) * keep_prev).astype(BF16)
    hn_scr[HALO:HALO + tm, :] = _rms_rows(xc_ref[...], gain).astype(BF16)
    hn_scr[HALO + tm:, :] = (_rms_rows(xn_ref[...], gain) * keep_next).astype(BF16)
    hn = hn_scr[HALO:HALO + tm, :]

    def gate_and_terms():
        gate_ref[...] = _dot(hn, wg_ref[...]).astype(BF16)
        ba = _dot(hn, wba_ref[...])
        g_c = -jnp.exp(alog_ref[...]) * _softplus(ba + dtb_ref[...])
        bat = _dot_nt(wbat_ref[...], hn)
        g_r = -jnp.exp(alogc_ref[...]) * _softplus(bat + dtbc_ref[...])
        return _sigmoid(ba), _sigmoid(bat), _split3(g_c), _split3(g_r)

    def decay_sums(beta_c, beta_r, gc3, gr3):
        ent_r = lax.broadcasted_iota(jnp.int32, (4 * B_HEADS, tm), 0)
        cum_r = jnp.where(ent_r < 3 * B_HEADS, sum(_dot(p, trilt_ref[...]) for p in gr3),
                          sum(_dot(p, triut_ref[...]) for p in gr3))
        row_ref[...] = jnp.where(ent_r < 2 * B_HEADS, beta_r, cum_r)
        ent_c = lax.broadcasted_iota(jnp.int32, (CHUNK, 4 * B_HEADS), 1)
        for c in range(tm // CHUNK):
            rows = slice(c * CHUNK, (c + 1) * CHUNK)
            fwd = sum(_dot(tril_ref[...], p[rows]) for p in gc3)
            bwd = sum(_dot(triu_ref[...], p[rows]) for p in gc3)
            col_ref[rows, :] = jnp.where(ent_c < 2 * B_HEADS, beta_c[rows], jnp.where(ent_c < 3 * B_HEADS, fwd, bwd))

    taps = conv_ref[...]
    hn_ext = hn_scr[...]
    n_blocks = N_QKV // PROJ_BLOCK
    rc = 128

    def project(n):
        proj_scr[n % PROJ_RING] = _dot(hn_ext, wqkv_ref[:, n * PROJ_BLOCK:(n + 1) * PROJ_BLOCK])

    def conv(n):
        for half in range(PROJ_BLOCK // 128):
            cb = n * (PROJ_BLOCK // 128) + half
            cols = slice(cb * 128, (cb + 1) * 128)
            out_ref, head = ((q_ref, k_ref, v_ref)[cb // B_HEADS], cb % B_HEADS)
            for r0 in range(0, tm, rc):
                window = proj_scr[n % PROJ_RING, r0:r0 + rc + 2 * HALO, half * 128:(half + 1) * 128]
                acc = None
                for t in range(CONV_WIDTH):
                    shift = CONV_PAD - t
                    moved = window if shift == 0 else pltpu.roll(window, shift % (rc + 2 * HALO), axis=0)
                    term = taps[t:t + 1, cols] * moved[HALO:HALO + rc]
                    acc = term if acc is None else acc + term
                y = _silu(acc)
                if out_ref is not v_ref:
                    scale = 1.0 if out_ref is k_ref else B_HEAD_K ** -0.5
                    y = y * (lax.rsqrt(jnp.sum(y * y, axis=-1, keepdims=True) + EPS) * scale)
                out_ref[r0:r0 + rc, head * 128:(head + 1) * 128] = y.astype(BF16)

    project(0)
    terms = gate_and_terms()
    yield
    for n in range(n_blocks):
        if n + 1 < n_blocks:
            project(n + 1)
        yield
        conv(n)
        if n == 1:
            decay_sums(*terms)


def _delta_front_kernel(*refs, tm, n_tiles, n_total):
    n_in, n_out = 17, 7
    ins, outs, scr = refs[:n_in], refs[n_in:n_in + n_out], refs[n_in + n_out:]
    gate_ref, w_ref, u_ref, qg_ref, kd_ref, qk_ref, egl_ref = outs
    hn_scr, proj_scr, q_scr, k_scr, v_scr, col_scr, row_scr = scr
    j = pl.program_id(0)

    @pl.when(j == 0)
    def _():
        for ref in (q_scr, k_scr, v_scr, col_scr, row_scr):
            ref[1] = jnp.zeros(ref.shape[1:], ref.dtype)

    cur = j % 2
    prev = 1 - cur
    tile = lax.rem(jnp.minimum(j, n_total - 1), n_tiles)
    proj = _proj_stages(tile, *ins, q_scr.at[cur], k_scr.at[cur], v_scr.at[cur], gate_ref, col_scr.at[cur],
                        row_scr.at[cur], hn_scr, proj_scr, tm=tm, n_tiles=n_tiles)
    n_chunks = tm // CHUNK
    waves = [range(c0, c0 + PREP_WAVE) for c0 in range(0, n_chunks, PREP_WAVE)]
    prep = itertools.chain.from_iterable(
        _prep_stages(q_scr.at[prev], k_scr.at[prev], v_scr.at[prev], col_scr.at[prev], row_scr.at[prev],
                     w_ref, u_ref, qg_ref, kd_ref, qk_ref, egl_ref, chunk_ids=wave) for wave in waves)
    live = {"proj": proj, "prep": prep}
    turn = 0
    while live:
        for name in ("proj", "prep", "prep") if turn % 2 else ("proj", "prep"):
            if name in live and next(live[name], StopIteration) is StopIteration:
                del live[name]
        turn += 1


def _delta_front(x1, gain, w_in, conv_w, a_log, dt_bias):
    b, s, _ = x1.shape
    tm = FRONT_TILE
    assert s % tm == 0
    n_tiles = s // tm
    nc = s // CHUNK
    per = tm // HALO
    nh = s // HALO
    kw = B_HEADS * B_HEAD_K
    wqkv = w_in[:, :N_QKV].astype(BF16)
    wg = w_in[:, N_QKV:N_QKV + kw].astype(BF16)
    wba = w_in[:, N_QKV + kw:].astype(BF16)
    wbat = wba.T
    pad = jnp.zeros((1, 2 * B_HEADS), F32)
    alog = jnp.concatenate([pad, a_log.reshape(1, 2 * B_HEADS).astype(F32)], axis=1)
    dtb = jnp.concatenate([pad, dt_bias.reshape(1, 2 * B_HEADS).astype(F32)], axis=1)
    i = np.arange(CHUNK)
    tril = (i[:, None] >= i[None, :]).astype(np.float32)
    triu = tril.T
    eye = np.eye(tm // CHUNK, dtype=np.float32)
    trilt = jnp.asarray(np.kron(eye, tril.T), BF16)
    triut = jnp.asarray(np.kron(eye, triu.T), BF16)
    tril = jnp.asarray(tril, BF16)
    triu = jnp.asarray(triu, BF16)
    full = lambda a: pl.BlockSpec(a.shape, lambda j_: (0,) * a.ndim, pipeline_mode=pl.Buffered(1))
    n_total = b * n_tiles
    cur = lambda j_: divmod(jnp.minimum(j_, n_total - 1), n_tiles)
    done = lambda j_: divmod(jnp.maximum(j_ - 1, 0), n_tiles)
    both = lambda n: pl.BlockSpec((None, 2, tm, n), lambda j_: (done(j_)[0], 0, done(j_)[1], 0))
    act = jax.ShapeDtypeStruct((b, 2, s, kw), BF16)
    kernel = functools.partial(_delta_front_kernel, tm=tm, n_tiles=n_tiles, n_total=n_total)
    consts = [gain, wqkv, wg, wba, wbat, conv_w.astype(F32), alog, dtb, alog.T, dtb.T, tril, triu, trilt, triut]
    return pl.pallas_call(
        kernel,
        grid=(n_total + 1,),
        in_specs=[
            pl.BlockSpec((None, HALO, D_MODEL), lambda j_: (cur(j_)[0], jnp.maximum(cur(j_)[1] * per - 1, 0), 0)),
            pl.BlockSpec((None, tm, D_MODEL), lambda j_: (cur(j_)[0], cur(j_)[1], 0)),
            pl.BlockSpec((None, HALO, D_MODEL),
                         lambda j_: (cur(j_)[0], jnp.minimum((cur(j_)[1] + 1) * per, nh - 1), 0)),
        ] + [full(a) for a in consts],
        out_specs=[pl.BlockSpec((None, tm, kw), lambda j_: (cur(j_)[0], cur(j_)[1], 0)),
                   both(kw), both(kw), both(kw), both(kw), both(B_HEADS * CHUNK),
                   pl.BlockSpec((None, 2, tm // CHUNK, 1, kw), lambda j_: (done(j_)[0], 0, done(j_)[1], 0, 0))],
        out_shape=[jax.ShapeDtypeStruct((b, s, kw), BF16), act, act, act, act,
                   jax.ShapeDtypeStruct((b, 2, s, B_HEADS * CHUNK), BF16),
                   jax.ShapeDtypeStruct((b, 2, nc, 1, kw), F32)],
        scratch_shapes=[pltpu.VMEM((tm + 2 * HALO, D_MODEL), BF16),
                        pltpu.VMEM((PROJ_RING, tm + 2 * HALO, PROJ_BLOCK), F32),
                        pltpu.VMEM((2, tm, kw), BF16), pltpu.VMEM((2, tm, kw), BF16), pltpu.VMEM((2, tm, kw), BF16),
                        pltpu.VMEM((2, tm, 4 * B_HEADS), F32), pltpu.VMEM((2, 4 * B_HEADS, tm), F32)],
        compiler_params=_params(("arbitrary",)),
        name="delta_front",
    )(x1, x1, x1, *consts)


FRONT_TILE = 512
PREP_WAVE = 4
GROUP = 4
N_GROUPS = B_HEADS // GROUP
PAIR = 2
N_PAIRS = B_HEADS // PAIR


def _block_diag(x, n_blocks):
    r, width = x.shape
    shift = (width // n_blocks).bit_length() - 1
    blk = lax.broadcasted_iota(jnp.int32, (r, width), 1) >> shift
    zero = jnp.zeros_like(x)
    return jnp.concatenate([jnp.where(blk == a, x, zero) for a in range(n_blocks)], axis=0)


def _prep_stages(q_ref, k_ref, v_ref, col_ref, row_ref, w_ref, u_ref, qg_ref, kd_ref, qk_ref, egl_ref, *, chunk_ids):
    width = GROUP * CHUNK

    def row_piece(kind, c, g):
        first = kind * B_HEADS + g * GROUP
        return jnp.concatenate([row_ref[first + a:first + a + 1, c * CHUNK:(c + 1) * CHUNK] for a in range(GROUP)],
                               axis=1)

    ri = lax.broadcasted_iota(jnp.int32, (CHUNK, width), 0)
    cj = lax.broadcasted_iota(jnp.int32, (CHUNK, width), 1) & (CHUNK - 1)
    low_half = lax.broadcasted_iota(jnp.int32, (CHUNK, 128), 1) < CHUNK
    eye = (ri == cj).astype(F32)
    masks = ((ri >= cj, ri > cj, CHUNK - 1), (ri <= cj, ri < cj, 0))

    def side_by_side(parts):
        return jnp.concatenate([jnp.where(low_half, parts[0], parts[1]), jnp.where(low_half, parts[2], parts[3])],
                               axis=1)

    gram, kbd = {}, {}
    for c in chunk_ids:
        rows = slice(c * CHUNK, (c + 1) * CHUNK)
        for g in range(N_GROUPS):
            cols = slice(g * 512, (g + 1) * 512)
            k4 = k_ref[rows, cols]
            kbd[c, g] = _block_diag(k4, GROUP)
            gram[c, g] = _dot_nt(jnp.concatenate([q_ref[rows, cols], k4], axis=0), kbd[c, g])
    yield

    x, p, dq, dk = {}, {}, {}, {}
    is_eye = ri == cj
    for c in chunk_ids:
        rows = slice(c * CHUNK, (c + 1) * CHUNK)
        for d, (incl, strict, last) in enumerate(masks):
            for g in range(N_GROUPS):
                beta_cols, gc_cols, last_rows = [], [], []
                for a in range(GROUP):
                    h = g * GROUP + a
                    idx = d * B_HEADS + h
                    hcols = slice(h * 128, (h + 1) * 128)
                    beta_cols.append(jnp.broadcast_to(col_ref[rows, idx:idx + 1], (CHUNK, 128)))
                    gc_cols.append(jnp.broadcast_to(col_ref[rows, 2 * B_HEADS + idx:2 * B_HEADS + idx + 1],
                                                    (CHUNK, 128)))
                    t_last = c * CHUNK + last
                    g_last = row_ref[2 * B_HEADS + idx:2 * B_HEADS + idx + 1, t_last:t_last + 1]
                    egl_ref[d, c, :, hcols] = jnp.exp(jnp.broadcast_to(g_last, (1, 128)))
                    last_rows.append(jnp.broadcast_to(g_last, (1, CHUNK)))
                lanes = slice(g * width, (g + 1) * width)
                gc_row = row_piece(2 + d, c, g)
                gc_col = side_by_side(gc_cols)
                dq[c, d, g] = jnp.where(is_eye, jnp.exp(gc_col), 0.0).astype(BF16)
                dk[c, d, g] = jnp.where(is_eye, jnp.exp(jnp.concatenate(last_rows, axis=1) - gc_col), 0.0).astype(BF16)
                decay = jnp.where(incl, jnp.exp(jnp.where(incl, gc_col - gc_row, 0.0)), 0.0)
                gm = gram[c, g]
                qk_ref[d, rows, lanes] = (gm[:CHUNK] * decay).astype(BF16)
                low = jnp.where(strict, gm[CHUNK:] * decay * side_by_side(beta_cols), 0.0)
                x[c, d, g] = -low
                p[c, d, g] = eye - low
        if c % 2 == 1:
            yield

    keys = list(x)
    for key in keys:
        xb = x[key].astype(BF16)
        x[key] = _dot(xb, _block_diag(xb, GROUP))
    yield
    squarings = CHUNK.bit_length() - 2
    for _ in range(squarings - 1):
        for key in keys:
            xb = x[key].astype(BF16)
            z = _dot(jnp.concatenate([xb, p[key].astype(BF16)], axis=0), _block_diag(xb, GROUP))
            x[key] = z[:CHUNK]
            p[key] = p[key] + z[CHUNK:]
        yield
    for key in keys:
        p[key] = p[key] + _dot(p[key].astype(BF16), _block_diag(x[key].astype(BF16), GROUP))
    yield

    for c in chunk_ids:
        rows = slice(c * CHUNK, (c + 1) * CHUNK)
        for g in range(N_GROUPS):
            cols = slice(g * 512, (g + 1) * 512)
            tb = [p[c, d, g] * row_piece(d, c, g) for d in range(2)]
            tbg = [tb[d] * jnp.exp(row_piece(2 + d, c, g)) for d in range(2)]
            uu = _dot(jnp.concatenate(tb, axis=0).astype(BF16), _block_diag(v_ref[rows, cols], GROUP))
            k_lhs = jnp.concatenate([jnp.concatenate(tbg, axis=0).astype(BF16), dk[c, 0, g], dk[c, 1, g]], axis=0)
            ww = _dot(k_lhs, kbd[c, g])
            qq = _dot(jnp.concatenate([dq[c, 0, g], dq[c, 1, g]], axis=0), _block_diag(q_ref[rows, cols], GROUP))
            for d in range(2):
                u_ref[d, rows, cols] = uu[d * CHUNK:(d + 1) * CHUNK].astype(BF16)
                w_ref[d, rows, cols] = ww[d * CHUNK:(d + 1) * CHUNK].astype(BF16)
                kd_ref[d, rows, cols] = ww[(2 + d) * CHUNK:(3 + d) * CHUNK].astype(BF16)
                qg_ref[d, rows, cols] = qq[d * CHUNK:(d + 1) * CHUNK].astype(BF16)
    yield


def _delta_scan_kernel(*refs):
    per_dir = 6
    dirs = (refs[:per_dir] + (refs[2 * per_dir],), refs[per_dir:2 * per_dir] + (refs[2 * per_dir + 1],))
    state_ref = refs[2 * per_dir + 2]
    n_batch = dirs[0][0].shape[0]

    @pl.when(pl.program_id(0) == 0)
    def _():
        state_ref[...] = jnp.zeros_like(state_ref)

    tile = PAIR * B_HEAD_K
    units = [(i, d, hp) for i in range(n_batch) for d in range(2) for hp in range(N_PAIRS)]
    slot = lambda i, d, hp: (i * 2 + d) * N_PAIRS + hp
    rs = {}
    for i, d, hp in units:
        w_ref, _, qg_ref = dirs[d][:3]
        cols = slice(hp * tile, (hp + 1) * tile)
        lhs = jnp.concatenate([w_ref[i, :, cols], qg_ref[i, :, cols]], axis=0)
        rs[i, d, hp] = _dot(lhs, state_ref[slot(i, d, hp)].astype(BF16))
    for i, d, hp in units:
        _, u_ref, _, kd_ref, qk_ref, egl_ref, o_ref = dirs[d]
        cols = slice(hp * tile, (hp + 1) * tile)
        r = rs[i, d, hp]
        v_new = (u_ref[i, :, cols].astype(F32) - r[:CHUNK]).astype(BF16)
        qk = qk_ref[i, :, hp * PAIR * CHUNK:(hp + 1) * PAIR * CHUNK]
        o_ref[i, :, cols] = (r[CHUNK:] + _dot(qk, _block_diag(v_new, PAIR))).astype(o_ref.dtype)
        idx = slot(i, d, hp)
        for a in range(PAIR):
            blk = slice(a * B_HEAD_K, (a + 1) * B_HEAD_K)
            hcols = slice(hp * tile + a * B_HEAD_K, hp * tile + (a + 1) * B_HEAD_K)
            update = _dot_tn(kd_ref[i, :, hcols], v_new[:, blk])
            state_ref[idx, blk, blk] = state_ref[idx, blk, blk] * egl_ref[i, :, hcols] + update


def _delta_scan(w, u, qg, kd, qk, egl):
    b, _, s, kw = w.shape
    nc = s // CHUNK
    chunk_of = (lambda c: c, lambda c: nc - 1 - c)
    in_specs = []
    for d in range(2):
        act = lambda n, d=d: pl.BlockSpec((b, None, CHUNK, n), lambda c: (0, d, chunk_of[d](c), 0))
        in_specs += [act(kw), act(kw), act(kw), act(kw), act(B_HEADS * CHUNK),
                     pl.BlockSpec((b, None, None, 1, kw), lambda c, d=d: (0, d, chunk_of[d](c), 0, 0))]
    out = jax.ShapeDtypeStruct((b, s, kw), BF16)
    out_specs = [pl.BlockSpec((b, CHUNK, kw), lambda c, d=d: (0, chunk_of[d](c), 0)) for d in range(2)]
    args = (w, u, qg, kd, qk, egl)
    return pl.pallas_call(
        _delta_scan_kernel,
        grid=(nc,),
        in_specs=in_specs,
        out_specs=out_specs,
        out_shape=[out, out],
        scratch_shapes=[pltpu.VMEM((b * 2 * N_PAIRS, PAIR * B_HEAD_K, PAIR * B_HEAD_V), F32)],
        compiler_params=_params(("arbitrary",)),
        name="delta_scan",
    )(*args, *args)


def _delta_out_kernel(of_ref, ob_ref, gate_ref, x_ref, onorm_ref, w_ref, fnorm_ref, o_ref, og_scr):
    onorm = onorm_ref[...]
    for h in range(B_HEADS):
        cols = slice(h * 128, (h + 1) * 128)
        o = of_ref[:, cols].astype(F32) + ob_ref[:, cols].astype(F32)
        o = o * lax.rsqrt(jnp.mean(o * o, axis=-1, keepdims=True) + EPS) * onorm
        gate = gate_ref[:, cols].astype(F32)
        og_scr[:, cols] = (o * _silu(gate)).astype(BF16)
    y = x_ref[...] + _dot(og_scr[...], w_ref[...])
    o_ref[...] = _rms_rows(y, fnorm_ref[...])


def _delta_out(o_f, o_b, gate, x1, out_norm, w_out, final_norm):
    t = x1.shape[0]
    tm = 512
    assert t % tm == 0
    row = pl.BlockSpec((tm, D_MODEL), lambda i: (i, 0))
    vec = lambda n: pl.BlockSpec((1, n), lambda i: (0, 0))
    return pl.pallas_call(
        _delta_out_kernel,
        grid=(t // tm,),
        in_specs=[row, row, row, row, vec(B_HEAD_V), pl.BlockSpec(w_out.shape, lambda i: (0, 0)), vec(D_MODEL)],
        out_specs=row,
        out_shape=jax.ShapeDtypeStruct((t, D_MODEL), F32),
        scratch_shapes=[pltpu.VMEM((tm, D_MODEL), BF16)],
        compiler_params=_params(("parallel",)),
        name="delta_out",
    )(o_f, o_b, gate, x1, out_norm, w_out, final_norm)


def kernel(x, attn_norm, attn_w_in, attn_sink, attn_w_out, delta_norm, delta_w_in, delta_conv, delta_a_log,
           delta_dt_bias, delta_out_norm, delta_w_out, final_norm):
    b, s, d = x.shape
    assert d == D_MODEL
    t = b * s
    q_t, k, v_t, gate_t = _attn_proj(x, attn_norm.reshape(1, d).astype(F32), attn_w_in)
    x1 = _attn_core(x, q_t, k, v_t, gate_t, attn_sink.astype(F32), attn_w_out.astype(BF16))
    dgate, *operators = _delta_front(x1, delta_norm.reshape(1, d).astype(F32), delta_w_in, delta_conv,
                                     delta_a_log, delta_dt_bias)
    o_f, o_b = _delta_scan(*operators)
    out = _delta_out(o_f.reshape(t, d), o_b.reshape(t, d), dgate.reshape(t, d), x1.reshape(t, d),
                     delta_out_norm.reshape(1, B_HEAD_V).astype(F32), delta_w_out.astype(BF16),
                     final_norm.reshape(1, d).astype(F32))
    return out.reshape(b, s, d)
```

```python
import functools
import itertools

import numpy as np
import jax
import jax.numpy as jnp
from jax import lax
from jax.experimental import pallas as pl
from jax.experimental.pallas import tpu as pltpu

F32 = jnp.float32
BF16 = jnp.bfloat16
EPS = 1e-6
LOG2E = float(np.log2(np.e))

D_MODEL = 1024
A_HEADS = 16
A_KV_HEADS = 4
A_HEAD_DIM = 64
A_PAIRS = A_HEADS // 2
BLOCK = 128
ATTN_TILE = 512
KEYS = 3 * BLOCK

B_HEADS = 8
B_HEAD_K = 128
B_HEAD_V = 128
CONV_WIDTH = 5
CONV_PAD = CONV_WIDTH // 2
CHUNK = 64
N_QKV = 3 * B_HEADS * B_HEAD_K
HALO = 8
PROJ_BLOCK = 256
PROJ_RING = 3

V7X_VMEM_LIMIT_BYTES = 56 * 1024 * 1024


def _sigmoid(x):
    return 1.0 / (1.0 + jnp.exp(-x))


def _silu(x):
    half = 0.5 * x
    return half + half * jnp.tanh(half)


def _softplus(x):
    return jnp.maximum(x, 0.0) + jnp.log(1.0 + jnp.exp(-jnp.abs(x)))


def _rms_rows(x, gain):
    ms = jnp.mean(x * x, axis=-1, keepdims=True)
    return x * lax.rsqrt(ms + EPS) * gain


def _split3(x):
    a = x.astype(BF16)
    r = x - a.astype(F32)
    b = r.astype(BF16)
    c = (r - b.astype(F32)).astype(BF16)
    return a, b, c


def _dot(a, b):
    return jnp.dot(a, b, preferred_element_type=F32)


def _dot_nt(a, b):
    return lax.dot_general(a, b, (((1,), (1,)), ((), ())), preferred_element_type=F32)


def _dot_tn(a, b):
    return lax.dot_general(a, b, (((0,), (0,)), ((), ())), preferred_element_type=F32)


def _params(semantics):
    return pltpu.CompilerParams(dimension_semantics=semantics, vmem_limit_bytes=V7X_VMEM_LIMIT_BYTES)


def _attn_proj_kernel(x_ref, gain_ref, wqt_ref, wk_ref, wvt_ref, wgt_ref, qt_ref, k_ref, vt_ref, gt_ref):
    hn = _rms_rows(x_ref[...], gain_ref[...]).astype(BF16)
    qt_ref[...] = (_dot_nt(wqt_ref[...], hn) * (A_HEAD_DIM ** -0.5 * LOG2E)).astype(BF16)
    k_ref[...] = _dot(hn, wk_ref[...]).astype(BF16)
    vt_ref[...] = _dot_nt(wvt_ref[...], hn).astype(BF16)
    gt_ref[...] = _dot_nt(wgt_ref[...], hn).astype(BF16)


def _attn_proj(x, gain, w_in):
    b, s, _ = x.shape
    tm = ATTN_TILE
    assert s % tm == 0
    kvw = A_KV_HEADS * A_HEAD_DIM
    wqt = w_in[:, :D_MODEL].T.astype(BF16)
    wk = w_in[:, D_MODEL:D_MODEL + kvw].reshape(D_MODEL, A_KV_HEADS, A_HEAD_DIM)
    wk = jnp.concatenate([wk, jnp.zeros_like(wk)], axis=2).reshape(D_MODEL, 2 * kvw).astype(BF16)
    wvt = w_in[:, D_MODEL + kvw:D_MODEL + 2 * kvw].T.astype(BF16)
    wgt = w_in[:, D_MODEL + 2 * kvw:].T.astype(BF16)
    full = lambda a: pl.BlockSpec(a.shape, lambda i, j: (0, 0), pipeline_mode=pl.Buffered(1))
    feat = lambda n: pl.BlockSpec((None, None, n, tm), lambda i, j: (i, j, 0, 0))
    feat_shape = lambda n: jax.ShapeDtypeStruct((b, s // tm, n, tm), BF16)
    return pl.pallas_call(
        _attn_proj_kernel,
        grid=(b, s // tm),
        in_specs=[pl.BlockSpec((None, tm, D_MODEL), lambda i, j: (i, j, 0)), full(gain),
                  full(wqt), full(wk), full(wvt), full(wgt)],
        out_specs=[feat(D_MODEL), pl.BlockSpec((None, tm, 2 * kvw), lambda i, j: (i, j, 0)), feat(kvw), feat(D_MODEL)],
        out_shape=[feat_shape(D_MODEL), jax.ShapeDtypeStruct((b, s, 2 * kvw), BF16), feat_shape(kvw),
                   feat_shape(D_MODEL)],
        compiler_params=_params(("parallel", "parallel")),
        name="attn_proj",
    )(x, gain, wqt, wk, wvt, wgt)


A_GROUP = A_HEADS // A_KV_HEADS
ONES_ROWS = 16


def _attn_core_kernel(sink_ref, qt_ref, kp_ref, kc_ref, kn_ref, vp_ref, vc_ref, vn_ref, gt_ref, x_ref, bias_ref,
                      w_ref, o_ref, k_scr, vt_scr, *, tq, n_tiles):
    j = pl.program_id(1)
    k_scr[0:BLOCK, :] = kp_ref[...]
    k_scr[BLOCK:BLOCK + tq, :] = kc_ref[...]
    k_scr[BLOCK + tq:, :] = kn_ref[...]
    vt_scr[:, 0:BLOCK] = vp_ref[...]
    vt_scr[:, BLOCK:BLOCK + tq] = vc_ref[...]
    vt_scr[:, BLOCK + tq:] = vn_ref[...]
    zero_rows = jnp.zeros((A_HEAD_DIM, BLOCK), BF16)
    ones_rows = jnp.ones((ONES_ROWS, KEYS), BF16)
    n_qb = tq // BLOCK

    def scores(qb, g):
        keys = slice(qb * BLOCK, qb * BLOCK + KEYS)
        qcols = slice(qb * BLOCK, (qb + 1) * BLOCK)
        q_t = jnp.concatenate(
            [jnp.concatenate([qt_ref[(A_GROUP * g + r) * A_HEAD_DIM:(A_GROUP * g + r + 1) * A_HEAD_DIM, qcols],
                              zero_rows], axis=0) for r in range(A_GROUP)], axis=1)
        gqb = j * n_qb + qb
        table = jnp.where(gqb == 0, 1, jnp.where(gqb == n_tiles * n_qb - 1, 2, 0)) * A_KV_HEADS
        return _dot(k_scr[keys, g * 128:(g + 1) * 128], q_t) + bias_ref[table + g]

    def out_chunk(qb, og_t, c):
        rows = slice(qb * BLOCK, (qb + 1) * BLOCK)
        cols = slice(c * 256, (c + 1) * 256)
        o_ref[rows, cols] = x_ref[rows, cols] + _dot_tn(og_t, w_ref[:, cols])

    items = [(qb, g) for qb in range(n_qb) for g in range(A_KV_HEADS)]
    pending = []
    og_parts = []
    s_next = scores(*items[0])
    for n, (qb, g) in enumerate(items):
        keys = slice(qb * BLOCK, qb * BLOCK + KEYS)
        qcols = slice(qb * BLOCK, (qb + 1) * BLOCK)
        s = s_next
        if n + 1 < len(items):
            s_next = scores(*items[n + 1])
        if pending:
            pending.pop(0)()
        sink = sink_ref[g]
        m = jnp.maximum(jnp.max(s, axis=0, keepdims=True), sink)
        probs = jnp.exp2(s - m).astype(BF16)
        v_t = jnp.concatenate([vt_scr[g * A_HEAD_DIM:(g + 1) * A_HEAD_DIM, keys], ones_rows], axis=0)
        o_ext = _dot(v_t, probs)
        inv = 1.0 / (o_ext[A_HEAD_DIM:A_HEAD_DIM + 1] + jnp.exp2(sink - m))
        for r in range(A_GROUP):
            h = A_GROUP * g + r
            lanes = slice(r * BLOCK, (r + 1) * BLOCK)
            gate = gt_ref[h * A_HEAD_DIM:(h + 1) * A_HEAD_DIM, qcols].astype(F32)
            og_parts.append((o_ext[:A_HEAD_DIM, lanes] * inv[:, lanes] * _silu(gate)).astype(BF16))
        if g == A_KV_HEADS - 1:
            og_t = jnp.concatenate(og_parts, axis=0)
            og_parts = []
            pending += [functools.partial(out_chunk, qb, og_t, c) for c in range(D_MODEL // 256)]
    for piece in pending:
        piece()


def _attn_tables():
    qpos = np.arange(BLOCK)[:, None]
    krel = np.arange(KEYS)[None, :] - BLOCK
    dist = np.abs(krel - qpos).astype(np.float32)
    slopes = np.power(2.0, -8.0 * np.arange(1, A_HEADS + 1) / A_HEADS).astype(np.float32)
    band = np.where(dist <= BLOCK, 0.0, -np.inf)
    per_head = (-slopes[:, None, None] * dist[None]).astype(np.float64) * LOG2E + band[None]
    sets = np.broadcast_to(per_head[None], (3, A_HEADS, BLOCK, KEYS)).copy()
    sets[1, :, :, :BLOCK] = -np.inf
    sets[2, :, :, 2 * BLOCK:] = -np.inf
    sets = sets.reshape(3, A_KV_HEADS, A_GROUP, BLOCK, KEYS).transpose(0, 1, 4, 2, 3)
    return jnp.asarray(sets.reshape(3 * A_KV_HEADS, KEYS, A_GROUP * BLOCK), F32)


def _attn_core(x, q_t, k, v_t, gate_t, sink, w_out):
    b, s, _ = x.shape
    tq = ATTN_TILE
    assert s % tq == 0 and s // BLOCK >= 2
    n_tiles = s // tq
    nb = s // BLOCK
    per = tq // BLOCK
    kw = k.shape[-1]
    vw = v_t.shape[2]
    bias = _attn_tables()
    sink_rows = jnp.repeat((sink * LOG2E).reshape(A_KV_HEADS, 1, A_GROUP), BLOCK, axis=2)
    tile = pl.BlockSpec((None, tq, D_MODEL), lambda i, j: (i, j, 0))
    feat = lambda n: pl.BlockSpec((None, None, n, tq), lambda i, j: (i, j, 0, 0))
    prev = lambda j: jnp.maximum(j * per - 1, 0)
    nxt = lambda j: jnp.minimum((j + 1) * per, nb - 1)
    const = lambda a: pl.BlockSpec(a.shape, lambda i, j: (0,) * a.ndim, pipeline_mode=pl.Buffered(1))
    kernel = functools.partial(_attn_core_kernel, tq=tq, n_tiles=n_tiles)
    return pl.pallas_call(
        kernel,
        grid=(b, n_tiles),
        in_specs=[
            const(sink_rows),
            feat(D_MODEL),
            pl.BlockSpec((None, BLOCK, kw), lambda i, j: (i, prev(j), 0)),
            pl.BlockSpec((None, tq, kw), lambda i, j: (i, j, 0)),
            pl.BlockSpec((None, BLOCK, kw), lambda i, j: (i, nxt(j), 0)),
            pl.BlockSpec((None, None, vw, BLOCK), lambda i, j: (i, jnp.maximum(j - 1, 0), 0, per - 1)),
            feat(vw),
            pl.BlockSpec((None, None, vw, BLOCK), lambda i, j: (i, jnp.minimum(j + 1, n_tiles - 1), 0, 0)),
            feat(D_MODEL),
            tile,
            const(bias),
            const(w_out),
        ],
        out_specs=tile,
        out_shape=jax.ShapeDtypeStruct(x.shape, F32),
        scratch_shapes=[pltpu.VMEM((tq + 2 * BLOCK, kw), BF16), pltpu.VMEM((vw, tq + 2 * BLOCK), BF16)],
        compiler_params=_params(("parallel", "parallel")),
        name="attn_core",
    )(sink_rows, q_t, k, k, k, v_t, v_t, v_t, gate_t, x, bias, w_out)


def _proj_stages(j, xp_ref, xc_ref, xn_ref, gain_ref, wqkv_ref, wg_ref, wba_ref, wbat_ref, conv_ref,
                 alog_ref, dtb_ref, alogc_ref, dtbc_ref, tril_ref, triu_ref, trilt_ref, triut_ref,
                 q_ref, k_ref, v_ref, gate_ref, col_ref, row_ref, hn_scr, proj_scr, *, tm, n_tiles):
    gain = gain_ref[...]
    keep_prev = (j > 0).astype(F32)
    keep_next = (j < n_tiles - 1).astype(F32)
    hn_scr[0:HALO, :] = (_rms_rows(xp_ref[...], gain) * keep_prev).astype(BF16)
    hn_scr[HALO:HALO + tm, :] = _rms_rows(xc_ref[...], gain).astype(BF16)
    hn_scr[HALO + tm:, :] = (_rms_rows(xn_ref[...], gain) * keep_next).astype(BF16)
    hn = hn_scr[HALO:HALO + tm, :]

    def gate_and_terms():
        gate_ref[...] = _dot(hn, wg_ref[...]).astype(BF16)
        ba = _dot(hn, wba_ref[...])
        g_c = -jnp.exp(alog_ref[...]) * _softplus(ba + dtb_ref[...])
        bat = _dot_nt(wbat_ref[...], hn)
        g_r = -jnp.exp(alogc_ref[...]) * _softplus(bat + dtbc_ref[...])
        return _sigmoid(ba), _sigmoid(bat), _split3(g_c), _split3(g_r)

    def decay_sums(beta_c, beta_r, gc3, gr3):
        ent_r = lax.broadcasted_iota(jnp.int32, (4 * B_HEADS, tm), 0)
        cum_r = jnp.where(ent_r < 3 * B_HEADS, sum(_dot(p, trilt_ref[...]) for p in gr3),
                          sum(_dot(p, triut_ref[...]) for p in gr3))
        row_ref[...] = jnp.where(ent_r < 2 * B_HEADS, beta_r, cum_r)
        ent_c = lax.broadcasted_iota(jnp.int32, (CHUNK, 4 * B_HEADS), 1)
        for c in range(tm // CHUNK):
            rows = slice(c * CHUNK, (c + 1) * CHUNK)
            fwd = sum(_dot(tril_ref[...], p[rows]) for p in gc3)
            bwd = sum(_dot(triu_ref[...], p[rows]) for p in gc3)
            col_ref[rows, :] = jnp.where(ent_c < 2 * B_HEADS, beta_c[rows], jnp.where(ent_c < 3 * B_HEADS, fwd, bwd))

    taps = conv_ref[...]
    hn_ext = hn_scr[...]
    n_blocks = N_QKV // PROJ_BLOCK
    rc = 128

    def project(n):
        proj_scr[n % PROJ_RING] = _dot(hn_ext, wqkv_ref[:, n * PROJ_BLOCK:(n + 1) * PROJ_BLOCK])

    def conv(n):
        for half in range(PROJ_BLOCK // 128):
            cb = n * (PROJ_BLOCK // 128) + half
            cols = slice(cb * 128, (cb + 1) * 128)
            out_ref, head = ((q_ref, k_ref, v_ref)[cb // B_HEADS], cb % B_HEADS)
            for r0 in range(0, tm, rc):
                window = proj_scr[n % PROJ_RING, r0:r0 + rc + 2 * HALO, half * 128:(half + 1) * 128]
                acc = None
                for t in range(CONV_WIDTH):
                    shift = CONV_PAD - t
                    moved = window if shift == 0 else pltpu.roll(window, shift % (rc + 2 * HALO), axis=0)
                    term = taps[t:t + 1, cols] * moved[HALO:HALO + rc]
                    acc = term if acc is None else acc + term
                y = _silu(acc)
                if out_ref is not v_ref:
                    scale = 1.0 if out_ref is k_ref else B_HEAD_K ** -0.5
                    y = y * (lax.rsqrt(jnp.sum(y * y, axis=-1, keepdims=True) + EPS) * scale)
                out_ref[r0:r0 + rc, head * 128:(head + 1) * 128] = y.astype(BF16)

    project(0)
    terms = gate_and_terms()
    yield
    for n in range(n_blocks):
        if n + 1 < n_blocks:
            project(n + 1)
        yield
        conv(n)
        if n == 1:
            decay_sums(*terms)


def _delta_front_kernel(*refs, tm, n_tiles, n_total):
    n_in, n_out = 17, 7
    ins, outs, scr = refs[:n_in], refs[n_in:n_in + n_out], refs[n_in + n_out:]
    gate_ref, w_ref, u_ref, qg_ref, kd_ref, qk_ref, egl_ref = outs
    hn_scr, proj_scr, q_scr, k_scr, v_scr, col_scr, row_scr = scr
    j = pl.program_id(0)

    @pl.when(j == 0)
    def _():
        for ref in (q_scr, k_scr, v_scr, col_scr, row_scr):
            ref[1] = jnp.zeros(ref.shape[1:], ref.dtype)

    cur = j % 2
    prev = 1 - cur
    tile = lax.rem(jnp.minimum(j, n_total - 1), n_tiles)
    proj = _proj_stages(tile, *ins, q_scr.at[cur], k_scr.at[cur], v_scr.at[cur], gate_ref, col_scr.at[cur],
                        row_scr.at[cur], hn_scr, proj_scr, tm=tm, n_tiles=n_tiles)
    n_chunks = tm // CHUNK
    waves = [range(c0, c0 + PREP_WAVE) for c0 in range(0, n_chunks, PREP_WAVE)]
    prep = itertools.chain.from_iterable(
        _prep_stages(q_scr.at[prev], k_scr.at[prev], v_scr.at[prev], col_scr.at[prev], row_scr.at[prev],
                     w_ref, u_ref, qg_ref, kd_ref, qk_ref, egl_ref, chunk_ids=wave) for wave in waves)
    live = {"proj": proj, "prep": prep}
    turn = 0
    while live:
        for name in ("proj", "prep", "prep") if turn % 2 else ("proj", "prep"):
            if name in live and next(live[name], StopIteration) is StopIteration:
                del live[name]
        turn += 1


def _delta_front(x1, gain, w_in, conv_w, a_log, dt_bias):
    b, s, _ = x1.shape
    tm = FRONT_TILE
    assert s % tm == 0
    n_tiles = s // tm
    nc = s // CHUNK
    per = tm // HALO
    nh = s // HALO
    kw = B_HEADS * B_HEAD_K
    wqkv = w_in[:, :N_QKV].astype(BF16)
    wg = w_in[:, N_QKV:N_QKV + kw].astype(BF16)
    wba = w_in[:, N_QKV + kw:].astype(BF16)
    wbat = wba.T
    pad = jnp.zeros((1, 2 * B_HEADS), F32)
    alog = jnp.concatenate([pad, a_log.reshape(1, 2 * B_HEADS).astype(F32)], axis=1)
    dtb = jnp.concatenate([pad, dt_bias.reshape(1, 2 * B_HEADS).astype(F32)], axis=1)
    i = np.arange(CHUNK)
    tril = (i[:, None] >= i[None, :]).astype(np.float32)
    triu = tril.T
    eye = np.eye(tm // CHUNK, dtype=np.float32)
    trilt = jnp.asarray(np.kron(eye, tril.T), BF16)
    triut = jnp.asarray(np.kron(eye, triu.T), BF16)
    tril = jnp.asarray(tril, BF16)
    triu = jnp.asarray(triu, BF16)
    full = lambda a: pl.BlockSpec(a.shape, lambda j_: (0,) * a.ndim, pipeline_mode=pl.Buffered(1))
    n_total = b * n_tiles
    cur = lambda j_: divmod(jnp.minimum(j_, n_total - 1), n_tiles)
    done = lambda j_: divmod(jnp.maximum(j_ - 1, 0), n_tiles)
    both = lambda n: pl.BlockSpec((None, 2, tm, n), lambda j_: (done(j_)[0], 0, done(j_)[1], 0))
    act = jax.ShapeDtypeStruct((b, 2, s, kw), BF16)
    kernel = functools.partial(_delta_front_kernel, tm=tm, n_tiles=n_tiles, n_total=n_total)
    consts = [gain, wqkv, wg, wba, wbat, conv_w.astype(F32), alog, dtb, alog.T, dtb.T, tril, triu, trilt, triut]
    return pl.pallas_call(
        kernel,
        grid=(n_total + 1,),
        in_specs=[
            pl.BlockSpec((None, HALO, D_MODEL), lambda j_: (cur(j_)[0], jnp.maximum(cur(j_)[1] * per - 1, 0), 0)),
            pl.BlockSpec((None, tm, D_MODEL), lambda j_: (cur(j_)[0], cur(j_)[1], 0)),
            pl.BlockSpec((None, HALO, D_MODEL),
                         lambda j_: (cur(j_)[0], jnp.minimum((cur(j_)[1] + 1) * per, nh - 1), 0)),
        ] + [full(a) for a in consts],
        out_specs=[pl.BlockSpec((None, tm, kw), lambda j_: (cur(j_)[0], cur(j_)[1], 0)),
                   both(kw), both(kw), both(kw), both(kw), both(B_HEADS * CHUNK),
                   pl.BlockSpec((None, 2, tm // CHUNK, 1, kw), lambda j_: (done(j_)[0], 0, done(j_)[1], 0, 0))],
        out_shape=[jax.ShapeDtypeStruct((b, s, kw), BF16), act, act, act, act,
                   jax.ShapeDtypeStruct((b, 2, s, B_HEADS * CHUNK), BF16),
                   jax.ShapeDtypeStruct((b, 2, nc, 1, kw), F32)],
        scratch_shapes=[pltpu.VMEM((tm + 2 * HALO, D_MODEL), BF16),
                        pltpu.VMEM((PROJ_RING, tm + 2 * HALO, PROJ_BLOCK), F32),
                        pltpu.VMEM((2, tm, kw), BF16), pltpu.VMEM((2, tm, kw), BF16), pltpu.VMEM((2, tm, kw), BF16),
                        pltpu.VMEM((2, tm, 4 * B_HEADS), F32), pltpu.VMEM((2, 4 * B_HEADS, tm), F32)],
        compiler_params=_params(("arbitrary",)),
        name="delta_front",
    )(x1, x1, x1, *consts)


FRONT_TILE = 512
PREP_WAVE = 4
GROUP = 4
N_GROUPS = B_HEADS // GROUP
PAIR = 2
N_PAIRS = B_HEADS // PAIR


def _block_diag(x, n_blocks):
    r, width = x.shape
    shift = (width // n_blocks).bit_length() - 1
    blk = lax.broadcasted_iota(jnp.int32, (r, width), 1) >> shift
    zero = jnp.zeros_like(x)
    return jnp.concatenate([jnp.where(blk == a, x, zero) for a in range(n_blocks)], axis=0)


def _prep_stages(q_ref, k_ref, v_ref, col_ref, row_ref, w_ref, u_ref, qg_ref, kd_ref, qk_ref, egl_ref, *, chunk_ids):
    width = GROUP * CHUNK

    def row_piece(kind, c, g):
        first = kind * B_HEADS + g * GROUP
        return jnp.concatenate([row_ref[first + a:first + a + 1, c * CHUNK:(c + 1) * CHUNK] for a in range(GROUP)],
                               axis=1)

    ri = lax.broadcasted_iota(jnp.int32, (CHUNK, width), 0)
    cj = lax.broadcasted_iota(jnp.int32, (CHUNK, width), 1) & (CHUNK - 1)
    low_half = lax.broadcasted_iota(jnp.int32, (CHUNK, 128), 1) < CHUNK
    eye = (ri == cj).astype(F32)
    masks = ((ri >= cj, ri > cj, CHUNK - 1), (ri <= cj, ri < cj, 0))

    def side_by_side(parts):
        return jnp.concatenate([jnp.where(low_half, parts[0], parts[1]), jnp.where(low_half, parts[2], parts[3])],
                               axis=1)

    gram, kbd = {}, {}
    for c in chunk_ids:
        rows = slice(c * CHUNK, (c + 1) * CHUNK)
        for g in range(N_GROUPS):
            cols = slice(g * 512, (g + 1) * 512)
            k4 = k_ref[rows, cols]
            kbd[c, g] = _block_diag(k4, GROUP)
            gram[c, g] = _dot_nt(jnp.concatenate([q_ref[rows, cols], k4], axis=0), kbd[c, g])
    yield

    x, p, dq, dk = {}, {}, {}, {}
    is_eye = ri == cj
    for c in chunk_ids:
        rows = slice(c * CHUNK, (c + 1) * CHUNK)
        for d, (incl, strict, last) in enumerate(masks):
            for g in range(N_GROUPS):
                beta_cols, gc_cols, last_rows = [], [], []
                for a in range(GROUP):
                    h = g * GROUP + a
                    idx = d * B_HEADS + h
                    hcols = slice(h * 128, (h + 1) * 128)
                    beta_cols.append(jnp.broadcast_to(col_ref[rows, idx:idx + 1], (CHUNK, 128)))
                    gc_cols.append(jnp.broadcast_to(col_ref[rows, 2 * B_HEADS + idx:2 * B_HEADS + idx + 1],
                                                    (CHUNK, 128)))
                    t_last = c * CHUNK + last
                    g_last = row_ref[2 * B_HEADS + idx:2 * B_HEADS + idx + 1, t_last:t_last + 1]
                    egl_ref[d, c, :, hcols] = jnp.exp(jnp.broadcast_to(g_last, (1, 128)))
                    last_rows.append(jnp.broadcast_to(g_last, (1, CHUNK)))
                lanes = slice(g * width, (g + 1) * width)
                gc_row = row_piece(2 + d, c, g)
                gc_col = side_by_side(gc_cols)
                dq[c, d, g] = jnp.where(is_eye, jnp.exp(gc_col), 0.0).astype(BF16)
                dk[c, d, g] = jnp.where(is_eye, jnp.exp(jnp.concatenate(last_rows, axis=1) - gc_col), 0.0).astype(BF16)
                decay = jnp.where(incl, jnp.exp(jnp.where(incl, gc_col - gc_row, 0.0)), 0.0)
                gm = gram[c, g]
                qk_ref[d, rows, lanes] = (gm[:CHUNK] * decay).astype(BF16)
                low = jnp.where(strict, gm[CHUNK:] * decay * side_by_side(beta_cols), 0.0)
                x[c, d, g] = -low
                p[c, d, g] = eye - low
        if c % 2 == 1:
            yield

    keys = list(x)
    for key in keys:
        xb = x[key].astype(BF16)
        x[key] = _dot(xb, _block_diag(xb, GROUP))
    yield
    squarings = CHUNK.bit_length() - 2
    for _ in range(squarings - 1):
        for key in keys:
            xb = x[key].astype(BF16)
            z = _dot(jnp.concatenate([xb, p[key].astype(BF16)], axis=0), _block_diag(xb, GROUP))
            x[key] = z[:CHUNK]
            p[key] = p[key] + z[CHUNK:]
        yield
    for key in keys:
        p[key] = p[key] + _dot(p[key].astype(BF16), _block_diag(x[key].astype(BF16), GROUP))
    yield

    for c in chunk_ids:
        rows = slice(c * CHUNK, (c + 1) * CHUNK)
        for g in range(N_GROUPS):
            cols = slice(g * 512, (g + 1) * 512)
            tb = [p[c, d, g] * row_piece(d, c, g) for d in range(2)]
            tbg = [tb[d] * jnp.exp(row_piece(2 + d, c, g)) for d in range(2)]
            uu = _dot(jnp.concatenate(tb, axis=0).astype(BF16), _block_diag(v_ref[rows, cols], GROUP))
            k_lhs = jnp.concatenate([jnp.concatenate(tbg, axis=0).astype(BF16), dk[c, 0, g], dk[c, 1, g]], axis=0)
            ww = _dot(k_lhs, kbd[c, g])
            qq = _dot(jnp.concatenate([dq[c, 0, g], dq[c, 1, g]], axis=0), _block_diag(q_ref[rows, cols], GROUP))
            for d in range(2):
                u_ref[d, rows, cols] = uu[d * CHUNK:(d + 1) * CHUNK].astype(BF16)
                w_ref[d, rows, cols] = ww[d * CHUNK:(d + 1) * CHUNK].astype(BF16)
                kd_ref[d, rows, cols] = ww[(2 + d) * CHUNK:(3 + d) * CHUNK].astype(BF16)
                qg_ref[d, rows, cols] = qq[d * CHUNK:(d + 1) * CHUNK].astype(BF16)
    yield


SCAN_CHUNKS = 2


def _delta_scan_kernel(*refs):
    per_dir = 6
    dirs = (refs[:per_dir] + (refs[2 * per_dir],), refs[per_dir:2 * per_dir] + (refs[2 * per_dir + 1],))
    state_ref = refs[2 * per_dir + 2]
    n_batch = dirs[0][0].shape[0]

    @pl.when(pl.program_id(0) == 0)
    def _():
        state_ref[...] = jnp.zeros_like(state_ref)

    tile = PAIR * B_HEAD_K
    units = [(i, d, hp) for i in range(n_batch) for d in range(2) for hp in range(N_PAIRS)]
    slot = lambda i, d, hp: (i * 2 + d) * N_PAIRS + hp
    for step in range(SCAN_CHUNKS):
        sub = (step, SCAN_CHUNKS - 1 - step)
        rows = [slice(c * CHUNK, (c + 1) * CHUNK) for c in sub]
        rs = {}
        for i, d, hp in units:
            w_ref, _, qg_ref = dirs[d][:3]
            cols = slice(hp * tile, (hp + 1) * tile)
            lhs = jnp.concatenate([w_ref[i, rows[d], cols], qg_ref[i, rows[d], cols]], axis=0)
            rs[i, d, hp] = _dot(lhs, state_ref[slot(i, d, hp)].astype(BF16))
        for i, d, hp in units:
            _, u_ref, _, kd_ref, qk_ref, egl_ref, o_ref = dirs[d]
            cols = slice(hp * tile, (hp + 1) * tile)
            r = rs[i, d, hp]
            v_new = (u_ref[i, rows[d], cols].astype(F32) - r[:CHUNK]).astype(BF16)
            qk = qk_ref[i, rows[d], hp * PAIR * CHUNK:(hp + 1) * PAIR * CHUNK]
            o_ref[i, rows[d], cols] = (r[CHUNK:] + _dot(qk, _block_diag(v_new, PAIR))).astype(o_ref.dtype)
            idx = slot(i, d, hp)
            for a in range(PAIR):
                blk = slice(a * B_HEAD_K, (a + 1) * B_HEAD_K)
                hcols = slice(hp * tile + a * B_HEAD_K, hp * tile + (a + 1) * B_HEAD_K)
                update = _dot_tn(kd_ref[i, rows[d], hcols], v_new[:, blk])
                state_ref[idx, blk, blk] = state_ref[idx, blk, blk] * egl_ref[i, sub[d], :, hcols] + update


def _delta_scan(w, u, qg, kd, qk, egl):
    b, _, s, kw = w.shape
    rows = SCAN_CHUNKS * CHUNK
    assert s % rows == 0
    n_steps = s // rows
    chunk_of = (lambda c: c, lambda c: n_steps - 1 - c)
    in_specs = []
    for d in range(2):
        act = lambda n, d=d: pl.BlockSpec((b, None, rows, n), lambda c: (0, d, chunk_of[d](c), 0))
        in_specs += [act(kw), act(kw), act(kw), act(kw), act(B_HEADS * CHUNK),
                     pl.BlockSpec((b, None, SCAN_CHUNKS, 1, kw), lambda c, d=d: (0, d, chunk_of[d](c), 0, 0))]
    out = jax.ShapeDtypeStruct((b, s, kw), BF16)
    out_specs = [pl.BlockSpec((b, rows, kw), lambda c, d=d: (0, chunk_of[d](c), 0)) for d in range(2)]
    args = (w, u, qg, kd, qk, egl)
    return pl.pallas_call(
        _delta_scan_kernel,
        grid=(n_steps,),
        in_specs=in_specs,
        out_specs=out_specs,
        out_shape=[out, out],
        scratch_shapes=[pltpu.VMEM((b * 2 * N_PAIRS, PAIR * B_HEAD_K, PAIR * B_HEAD_V), F32)],
        compiler_params=_params(("arbitrary",)),
        name="delta_scan",
    )(*args, *args)


def _delta_out_kernel(of_ref, ob_ref, gate_ref, x_ref, onorm_ref, w_ref, fnorm_ref, o_ref, og_scr):
    onorm = onorm_ref[...]
    for h in range(B_HEADS):
        cols = slice(h * 128, (h + 1) * 128)
        o = of_ref[:, cols].astype(F32) + ob_ref[:, cols].astype(F32)
        o = o * lax.rsqrt(jnp.mean(o * o, axis=-1, keepdims=True) + EPS) * onorm
        gate = gate_ref[:, cols].astype(F32)
        og_scr[:, cols] = (o * _silu(gate)).astype(BF16)
    y = x_ref[...] + _dot(og_scr[...], w_ref[...])
    o_ref[...] = _rms_rows(y, fnorm_ref[...])


def _delta_out(o_f, o_b, gate, x1, out_norm, w_out, final_norm):
    t = x1.shape[0]
    tm = 512
    assert t % tm == 0
    row = pl.BlockSpec((tm, D_MODEL), lambda i: (i, 0))
    vec = lambda n: pl.BlockSpec((1, n), lambda i: (0, 0))
    return pl.pallas_call(
        _delta_out_kernel,
        grid=(t // tm,),
        in_specs=[row, row, row, row, vec(B_HEAD_V), pl.BlockSpec(w_out.shape, lambda i: (0, 0)), vec(D_MODEL)],
        out_specs=row,
        out_shape=jax.ShapeDtypeStruct((t, D_MODEL), F32),
        scratch_shapes=[pltpu.VMEM((tm, D_MODEL), BF16)],
        compiler_params=_params(("parallel",)),
        name="delta_out",
    )(o_f, o_b, gate, x1, out_norm, w_out, final_norm)


def kernel(x, attn_norm, attn_w_in, attn_sink, attn_w_out, delta_norm, delta_w_in, delta_conv, delta_a_log,
           delta_dt_bias, delta_out_norm, delta_w_out, final_norm):
    b, s, d = x.shape
    assert d == D_MODEL
    t = b * s
    q_t, k, v_t, gate_t = _attn_proj(x, attn_norm.reshape(1, d).astype(F32), attn_w_in)
    x1 = _attn_core(x, q_t, k, v_t, gate_t, attn_sink.astype(F32), attn_w_out.astype(BF16))
    dgate, *operators = _delta_front(x1, delta_norm.reshape(1, d).astype(F32), delta_w_in, delta_conv,
                                     delta_a_log, delta_dt_bias)
    o_f, o_b = _delta_scan(*operators)
    out = _delta_out(o_f.reshape(t, d), o_b.reshape(t, d), dgate.reshape(t, d), x1.reshape(t, d),
                     delta_out_norm.reshape(1, B_HEAD_V).astype(F32), delta_w_out.astype(BF16),
                     final_norm.reshape(1, d).astype(F32))
    return out.reshape(b, s, d)
```

```python
import functools
import itertools

import numpy as np
import jax
import jax.numpy as jnp
from jax import lax
from jax.experimental import pallas as pl
from jax.experimental.pallas import tpu as pltpu

F32 = jnp.float32
BF16 = jnp.bfloat16
EPS = 1e-6
LOG2E = float(np.log2(np.e))

D_MODEL = 1024
A_HEADS = 16
A_KV_HEADS = 4
A_HEAD_DIM = 64
A_PAIRS = A_HEADS // 2
BLOCK = 128
ATTN_TILE = 512
KEYS = 3 * BLOCK

B_HEADS = 8
B_HEAD_K = 128
B_HEAD_V = 128
CONV_WIDTH = 5
CONV_PAD = CONV_WIDTH // 2
CHUNK = 64
N_QKV = 3 * B_HEADS * B_HEAD_K
HALO = 8
PROJ_BLOCK = 256
PROJ_RING = 3

V7X_VMEM_LIMIT_BYTES = 56 * 1024 * 1024


def _sigmoid(x):
    return 1.0 / (1.0 + jnp.exp(-x))


def _silu(x):
    half = 0.5 * x
    return half + half * jnp.tanh(half)


def _softplus(x):
    return jnp.maximum(x, 0.0) + jnp.log(1.0 + jnp.exp(-jnp.abs(x)))


def _rms_rows(x, gain):
    ms = jnp.mean(x * x, axis=-1, keepdims=True)
    return x * lax.rsqrt(ms + EPS) * gain


def _split3(x):
    a = x.astype(BF16)
    r = x - a.astype(F32)
    b = r.astype(BF16)
    c = (r - b.astype(F32)).astype(BF16)
    return a, b, c


def _dot(a, b):
    return jnp.dot(a, b, preferred_element_type=F32)


def _dot_nt(a, b):
    return lax.dot_general(a, b, (((1,), (1,)), ((), ())), preferred_element_type=F32)


def _dot_tn(a, b):
    return lax.dot_general(a, b, (((0,), (0,)), ((), ())), preferred_element_type=F32)


def _params(semantics):
    return pltpu.CompilerParams(dimension_semantics=semantics, vmem_limit_bytes=V7X_VMEM_LIMIT_BYTES)


def _attn_proj_kernel(x_ref, gain_ref, wqt_ref, wk_ref, wvt_ref, wgt_ref, qt_ref, k_ref, vt_ref, gt_ref):
    hn = _rms_rows(x_ref[...], gain_ref[...]).astype(BF16)
    qt_ref[...] = (_dot_nt(wqt_ref[...], hn) * (A_HEAD_DIM ** -0.5 * LOG2E)).astype(BF16)
    k_ref[...] = _dot(hn, wk_ref[...]).astype(BF16)
    vt_ref[...] = _dot_nt(wvt_ref[...], hn).astype(BF16)
    gt_ref[...] = _dot_nt(wgt_ref[...], hn).astype(BF16)


def _attn_proj(x, gain, w_in):
    b, s, _ = x.shape
    tm = ATTN_TILE
    assert s % tm == 0
    kvw = A_KV_HEADS * A_HEAD_DIM
    wqt = w_in[:, :D_MODEL].T.astype(BF16)
    wk = w_in[:, D_MODEL:D_MODEL + kvw].reshape(D_MODEL, A_KV_HEADS, A_HEAD_DIM)
    wk = jnp.concatenate([wk, jnp.zeros_like(wk)], axis=2).reshape(D_MODEL, 2 * kvw).astype(BF16)
    wvt = w_in[:, D_MODEL + kvw:D_MODEL + 2 * kvw].T.astype(BF16)
    wgt = w_in[:, D_MODEL + 2 * kvw:].T.astype(BF16)
    full = lambda a: pl.BlockSpec(a.shape, lambda i, j: (0, 0), pipeline_mode=pl.Buffered(1))
    feat = lambda n: pl.BlockSpec((None, None, n, tm), lambda i, j: (i, j, 0, 0))
    feat_shape = lambda n: jax.ShapeDtypeStruct((b, s // tm, n, tm), BF16)
    return pl.pallas_call(
        _attn_proj_kernel,
        grid=(b, s // tm),
        in_specs=[pl.BlockSpec((None, tm, D_MODEL), lambda i, j: (i, j, 0)), full(gain),
                  full(wqt), full(wk), full(wvt), full(wgt)],
        out_specs=[feat(D_MODEL), pl.BlockSpec((None, tm, 2 * kvw), lambda i, j: (i, j, 0)), feat(kvw), feat(D_MODEL)],
        out_shape=[feat_shape(D_MODEL), jax.ShapeDtypeStruct((b, s, 2 * kvw), BF16), feat_shape(kvw),
                   feat_shape(D_MODEL)],
        compiler_params=_params(("parallel", "parallel")),
        name="attn_proj",
    )(x, gain, wqt, wk, wvt, wgt)


A_GROUP = A_HEADS // A_KV_HEADS
ONES_ROWS = 16


def _attn_core_kernel(sink_ref, qt_ref, kp_ref, kc_ref, kn_ref, vp_ref, vc_ref, vn_ref, gt_ref, x_ref, bias_ref,
                      w_ref, o_ref, k_scr, vt_scr, *, tq, n_tiles):
    j = pl.program_id(1)
    k_scr[0:BLOCK, :] = kp_ref[...]
    k_scr[BLOCK:BLOCK + tq, :] = kc_ref[...]
    k_scr[BLOCK + tq:, :] = kn_ref[...]
    vt_scr[:, 0:BLOCK] = vp_ref[...]
    vt_scr[:, BLOCK:BLOCK + tq] = vc_ref[...]
    vt_scr[:, BLOCK + tq:] = vn_ref[...]
    zero_rows = jnp.zeros((A_HEAD_DIM, BLOCK), BF16)
    ones_rows = jnp.ones((ONES_ROWS, KEYS), BF16)
    n_qb = tq // BLOCK

    def scores(qb, g):
        keys = slice(qb * BLOCK, qb * BLOCK + KEYS)
        qcols = slice(qb * BLOCK, (qb + 1) * BLOCK)
        q_t = jnp.concatenate(
            [jnp.concatenate([qt_ref[(A_GROUP * g + r) * A_HEAD_DIM:(A_GROUP * g + r + 1) * A_HEAD_DIM, qcols],
                              zero_rows], axis=0) for r in range(A_GROUP)], axis=1)
        gqb = j * n_qb + qb
        table = jnp.where(gqb == 0, 1, jnp.where(gqb == n_tiles * n_qb - 1, 2, 0)) * A_KV_HEADS
        return _dot(k_scr[keys, g * 128:(g + 1) * 128], q_t) + bias_ref[table + g]

    def out_chunk(qb, og_t, c):
        rows = slice(qb * BLOCK, (qb + 1) * BLOCK)
        cols = slice(c * 256, (c + 1) * 256)
        o_ref[rows, cols] = x_ref[rows, cols] + _dot_tn(og_t, w_ref[:, cols])

    items = [(qb, g) for qb in range(n_qb) for g in range(A_KV_HEADS)]
    pending = []
    og_parts = []
    s_next = scores(*items[0])
    for n, (qb, g) in enumerate(items):
        keys = slice(qb * BLOCK, qb * BLOCK + KEYS)
        qcols = slice(qb * BLOCK, (qb + 1) * BLOCK)
        s = s_next
        if n + 1 < len(items):
            s_next = scores(*items[n + 1])
        if pending:
            pending.pop(0)()
        sink = sink_ref[g]
        m = jnp.maximum(jnp.max(s, axis=0, keepdims=True), sink)
        probs = jnp.exp2(s - m).astype(BF16)
        v_t = jnp.concatenate([vt_scr[g * A_HEAD_DIM:(g + 1) * A_HEAD_DIM, keys], ones_rows], axis=0)
        o_ext = _dot(v_t, probs)
        inv = 1.0 / (o_ext[A_HEAD_DIM:A_HEAD_DIM + 1] + jnp.exp2(sink - m))
        for r in range(A_GROUP):
            h = A_GROUP * g + r
            lanes = slice(r * BLOCK, (r + 1) * BLOCK)
            gate = gt_ref[h * A_HEAD_DIM:(h + 1) * A_HEAD_DIM, qcols].astype(F32)
            og_parts.append((o_ext[:A_HEAD_DIM, lanes] * inv[:, lanes] * _silu(gate)).astype(BF16))
        if g == A_KV_HEADS - 1:
            og_t = jnp.concatenate(og_parts, axis=0)
            og_parts = []
            pending += [functools.partial(out_chunk, qb, og_t, c) for c in range(D_MODEL // 256)]
    for piece in pending:
        piece()


def _attn_tables():
    qpos = np.arange(BLOCK)[:, None]
    krel = np.arange(KEYS)[None, :] - BLOCK
    dist = np.abs(krel - qpos).astype(np.float32)
    slopes = np.power(2.0, -8.0 * np.arange(1, A_HEADS + 1) / A_HEADS).astype(np.float32)
    band = np.where(dist <= BLOCK, 0.0, -np.inf)
    per_head = (-slopes[:, None, None] * dist[None]).astype(np.float64) * LOG2E + band[None]
    sets = np.broadcast_to(per_head[None], (3, A_HEADS, BLOCK, KEYS)).copy()
    sets[1, :, :, :BLOCK] = -np.inf
    sets[2, :, :, 2 * BLOCK:] = -np.inf
    sets = sets.reshape(3, A_KV_HEADS, A_GROUP, BLOCK, KEYS).transpose(0, 1, 4, 2, 3)
    return jnp.asarray(sets.reshape(3 * A_KV_HEADS, KEYS, A_GROUP * BLOCK), F32)


def _attn_core(x, q_t, k, v_t, gate_t, sink, w_out):
    b, s, _ = x.shape
    tq = ATTN_TILE
    assert s % tq == 0 and s // BLOCK >= 2
    n_tiles = s // tq
    nb = s // BLOCK
    per = tq // BLOCK
    kw = k.shape[-1]
    vw = v_t.shape[2]
    bias = _attn_tables()
    sink_rows = jnp.repeat((sink * LOG2E).reshape(A_KV_HEADS, 1, A_GROUP), BLOCK, axis=2)
    tile = pl.BlockSpec((None, tq, D_MODEL), lambda i, j: (i, j, 0))
    feat = lambda n: pl.BlockSpec((None, None, n, tq), lambda i, j: (i, j, 0, 0))
    prev = lambda j: jnp.maximum(j * per - 1, 0)
    nxt = lambda j: jnp.minimum((j + 1) * per, nb - 1)
    const = lambda a: pl.BlockSpec(a.shape, lambda i, j: (0,) * a.ndim, pipeline_mode=pl.Buffered(1))
    kernel = functools.partial(_attn_core_kernel, tq=tq, n_tiles=n_tiles)
    return pl.pallas_call(
        kernel,
        grid=(b, n_tiles),
        in_specs=[
            const(sink_rows),
            feat(D_MODEL),
            pl.BlockSpec((None, BLOCK, kw), lambda i, j: (i, prev(j), 0)),
            pl.BlockSpec((None, tq, kw), lambda i, j: (i, j, 0)),
            pl.BlockSpec((None, BLOCK, kw), lambda i, j: (i, nxt(j), 0)),
            pl.BlockSpec((None, None, vw, BLOCK), lambda i, j: (i, jnp.maximum(j - 1, 0), 0, per - 1)),
            feat(vw),
            pl.BlockSpec((None, None, vw, BLOCK), lambda i, j: (i, jnp.minimum(j + 1, n_tiles - 1), 0, 0)),
            feat(D_MODEL),
            tile,
            const(bias),
            const(w_out),
        ],
        out_specs=tile,
        out_shape=jax.ShapeDtypeStruct(x.shape, F32),
        scratch_shapes=[pltpu.VMEM((tq + 2 * BLOCK, kw), BF16), pltpu.VMEM((vw, tq + 2 * BLOCK), BF16)],
        compiler_params=_params(("parallel", "parallel")),
        name="attn_core",
    )(sink_rows, q_t, k, k, k, v_t, v_t, v_t, gate_t, x, bias, w_out)


def _proj_stages(j, xp_ref, xc_ref, xn_ref, gain_ref, wqkv_ref, wg_ref, wba_ref, wbat_ref, conv_ref,
                 alog_ref, dtb_ref, alogc_ref, dtbc_ref, tril_ref, triu_ref, trilt_ref, triut_ref,
                 q_ref, k_ref, v_ref, gate_ref, col_ref, row_ref, hn_scr, proj_scr, *, tm, n_tiles):
    gain = gain_ref[...]
    keep_prev = (j > 0).astype(F32)
    keep_next = (j < n_tiles - 1).astype(F32)
    hn_scr[0:HALO, :] = (_rms_rows(xp_ref[...], gain) * keep_prev).astype(BF16)
    hn_scr[HALO:HALO + tm, :] = _rms_rows(xc_ref[...], gain).astype(BF16)
    hn_scr[HALO + tm:, :] = (_rms_rows(xn_ref[...], gain) * keep_next).astype(BF16)
    hn = hn_scr[HALO:HALO + tm, :]

    def gate_and_terms():
        gate_ref[...] = _dot(hn, wg_ref[...]).astype(BF16)
        ba = _dot(hn, wba_ref[...])
        g_c = -jnp.exp(alog_ref[...]) * _softplus(ba + dtb_ref[...])
        bat = _dot_nt(wbat_ref[...], hn)
        g_r = -jnp.exp(alogc_ref[...]) * _softplus(bat + dtbc_ref[...])
        return _sigmoid(ba), _sigmoid(bat), _split3(g_c), _split3(g_r)

    def decay_sums(beta_c, beta_r, gc3, gr3):
        ent_r = lax.broadcasted_iota(jnp.int32, (4 * B_HEADS, tm), 0)
        cum_r = jnp.where(ent_r < 3 * B_HEADS, sum(_dot(p, trilt_ref[...]) for p in gr3),
                          sum(_dot(p, triut_ref[...]) for p in gr3))
        row_ref[...] = jnp.where(ent_r < 2 * B_HEADS, beta_r, cum_r)
        ent_c = lax.broadcasted_iota(jnp.int32, (CHUNK, 4 * B_HEADS), 1)
        for c in range(tm // CHUNK):
            rows = slice(c * CHUNK, (c + 1) * CHUNK)
            fwd = sum(_dot(tril_ref[...], p[rows]) for p in gc3)
            bwd = sum(_dot(triu_ref[...], p[rows]) for p in gc3)
            col_ref[rows, :] = jnp.where(ent_c < 2 * B_HEADS, beta_c[rows], jnp.where(ent_c < 3 * B_HEADS, fwd, bwd))

    taps = conv_ref[...]
    hn_ext = hn_scr[...]
    n_blocks = N_QKV // PROJ_BLOCK
    rc = 128

    def project(n):
        proj_scr[n % PROJ_RING] = _dot(hn_ext, wqkv_ref[:, n * PROJ_BLOCK:(n + 1) * PROJ_BLOCK])

    def conv(n):
        for half in range(PROJ_BLOCK // 128):
            cb = n * (PROJ_BLOCK // 128) + half
            cols = slice(cb * 128, (cb + 1) * 128)
            out_ref, head = ((q_ref, k_ref, v_ref)[cb // B_HEADS], cb % B_HEADS)
            for r0 in range(0, tm, rc):
                window = proj_scr[n % PROJ_RING, r0:r0 + rc + 2 * HALO, half * 128:(half + 1) * 128]
                acc = None
                for t in range(CONV_WIDTH):
                    shift = CONV_PAD - t
                    moved = window if shift == 0 else pltpu.roll(window, shift % (rc + 2 * HALO), axis=0)
                    term = taps[t:t + 1, cols] * moved[HALO:HALO + rc]
                    acc = term if acc is None else acc + term
                y = _silu(acc)
                if out_ref is not v_ref:
                    scale = 1.0 if out_ref is k_ref else B_HEAD_K ** -0.5
                    y = y * (lax.rsqrt(jnp.sum(y * y, axis=-1, keepdims=True) + EPS) * scale)
                out_ref[r0:r0 + rc, head * 128:(head + 1) * 128] = y.astype(BF16)

    project(0)
    terms = gate_and_terms()
    yield
    for n in range(n_blocks):
        if n + 1 < n_blocks:
            project(n + 1)
        yield
        conv(n)
        if n == 1:
            decay_sums(*terms)


def _delta_front_kernel(*refs, tm, n_tiles, n_total):
    n_in, n_out = 17, 7
    ins, outs, scr = refs[:n_in], refs[n_in:n_in + n_out], refs[n_in + n_out:]
    gate_ref, w_ref, u_ref, qg_ref, kd_ref, qk_ref, egl_ref = outs
    hn_scr, proj_scr, q_scr, k_scr, v_scr, col_scr, row_scr = scr
    j = pl.program_id(0)

    @pl.when(j == 0)
    def _():
        for ref in (q_scr, k_scr, v_scr, col_scr, row_scr):
            ref[1] = jnp.zeros(ref.shape[1:], ref.dtype)

    cur = j % 2
    prev = 1 - cur
    tile = lax.rem(jnp.minimum(j, n_total - 1), n_tiles)
    proj = _proj_stages(tile, *ins, q_scr.at[cur], k_scr.at[cur], v_scr.at[cur], gate_ref, col_scr.at[cur],
                        row_scr.at[cur], hn_scr, proj_scr, tm=tm, n_tiles=n_tiles)
    n_chunks = tm // CHUNK
    waves = [range(c0, c0 + PREP_WAVE) for c0 in range(0, n_chunks, PREP_WAVE)]
    prep = itertools.chain.from_iterable(
        _prep_stages(q_scr.at[prev], k_scr.at[prev], v_scr.at[prev], col_scr.at[prev], row_scr.at[prev],
                     w_ref, u_ref, qg_ref, kd_ref, qk_ref, egl_ref, chunk_ids=wave) for wave in waves)
    live = {"proj": proj, "prep": prep}
    turn = 0
    while live:
        for name in ("proj", "prep", "prep") if turn % 2 else ("proj", "prep"):
            if name in live and next(live[name], StopIteration) is StopIteration:
                del live[name]
        turn += 1


def _delta_front(x1, gain, w_in, conv_w, a_log, dt_bias):
    b, s, _ = x1.shape
    tm = FRONT_TILE
    assert s % tm == 0
    n_tiles = s // tm
    nc = s // CHUNK
    per = tm // HALO
    nh = s // HALO
    kw = B_HEADS * B_HEAD_K
    wqkv = w_in[:, :N_QKV].astype(BF16)
    wg = w_in[:, N_QKV:N_QKV + kw].astype(BF16)
    wba = w_in[:, N_QKV + kw:].astype(BF16)
    wbat = wba.T
    pad = jnp.zeros((1, 2 * B_HEADS), F32)
    alog = jnp.concatenate([pad, a_log.reshape(1, 2 * B_HEADS).astype(F32)], axis=1)
    dtb = jnp.concatenate([pad, dt_bias.reshape(1, 2 * B_HEADS).astype(F32)], axis=1)
    i = np.arange(CHUNK)
    tril = (i[:, None] >= i[None, :]).astype(np.float32)
    triu = tril.T
    eye = np.eye(tm // CHUNK, dtype=np.float32)
    trilt = jnp.asarray(np.kron(eye, tril.T), BF16)
    triut = jnp.asarray(np.kron(eye, triu.T), BF16)
    tril = jnp.asarray(tril, BF16)
    triu = jnp.asarray(triu, BF16)
    full = lambda a: pl.BlockSpec(a.shape, lambda j_: (0,) * a.ndim, pipeline_mode=pl.Buffered(1))
    n_total = b * n_tiles
    cur = lambda j_: divmod(jnp.minimum(j_, n_total - 1), n_tiles)
    done = lambda j_: divmod(jnp.maximum(j_ - 1, 0), n_tiles)
    both = lambda n: pl.BlockSpec((None, 2, tm, n), lambda j_: (done(j_)[0], 0, done(j_)[1], 0))
    act = jax.ShapeDtypeStruct((b, 2, s, kw), BF16)
    kernel = functools.partial(_delta_front_kernel, tm=tm, n_tiles=n_tiles, n_total=n_total)
    consts = [gain, wqkv, wg, wba, wbat, conv_w.astype(F32), alog, dtb, alog.T, dtb.T, tril, triu, trilt, triut]
    return pl.pallas_call(
        kernel,
        grid=(n_total + 1,),
        in_specs=[
            pl.BlockSpec((None, HALO, D_MODEL), lambda j_: (cur(j_)[0], jnp.maximum(cur(j_)[1] * per - 1, 0), 0)),
            pl.BlockSpec((None, tm, D_MODEL), lambda j_: (cur(j_)[0], cur(j_)[1], 0)),
            pl.BlockSpec((None, HALO, D_MODEL),
                         lambda j_: (cur(j_)[0], jnp.minimum((cur(j_)[1] + 1) * per, nh - 1), 0)),
        ] + [full(a) for a in consts],
        out_specs=[pl.BlockSpec((None, tm, kw), lambda j_: (cur(j_)[0], cur(j_)[1], 0)),
                   both(kw), both(kw), both(kw), both(kw), both(B_HEADS * CHUNK),
                   pl.BlockSpec((None, 2, tm // CHUNK, 1, kw), lambda j_: (done(j_)[0], 0, done(j_)[1], 0, 0))],
        out_shape=[jax.ShapeDtypeStruct((b, s, kw), BF16), act, act, act, act,
                   jax.ShapeDtypeStruct((b, 2, s, B_HEADS * CHUNK), BF16),
                   jax.ShapeDtypeStruct((b, 2, nc, 1, kw), F32)],
        scratch_shapes=[pltpu.VMEM((tm + 2 * HALO, D_MODEL), BF16),
                        pltpu.VMEM((PROJ_RING, tm + 2 * HALO, PROJ_BLOCK), F32),
                        pltpu.VMEM((2, tm, kw), BF16), pltpu.VMEM((2, tm, kw), BF16), pltpu.VMEM((2, tm, kw), BF16),
                        pltpu.VMEM((2, tm, 4 * B_HEADS), F32), pltpu.VMEM((2, 4 * B_HEADS, tm), F32)],
        compiler_params=_params(("arbitrary",)),
        name="delta_front",
    )(x1, x1, x1, *consts)


FRONT_TILE = 256
PREP_WAVE = 4
GROUP = 4
N_GROUPS = B_HEADS // GROUP
PAIR = 2
N_PAIRS = B_HEADS // PAIR


def _block_diag(x, n_blocks):
    r, width = x.shape
    shift = (width // n_blocks).bit_length() - 1
    blk = lax.broadcasted_iota(jnp.int32, (r, width), 1) >> shift
    zero = jnp.zeros_like(x)
    return jnp.concatenate([jnp.where(blk == a, x, zero) for a in range(n_blocks)], axis=0)


def _prep_stages(q_ref, k_ref, v_ref, col_ref, row_ref, w_ref, u_ref, qg_ref, kd_ref, qk_ref, egl_ref, *, chunk_ids):
    width = GROUP * CHUNK

    def row_piece(kind, c, g):
        first = kind * B_HEADS + g * GROUP
        return jnp.concatenate([row_ref[first + a:first + a + 1, c * CHUNK:(c + 1) * CHUNK] for a in range(GROUP)],
                               axis=1)

    ri = lax.broadcasted_iota(jnp.int32, (CHUNK, width), 0)
    cj = lax.broadcasted_iota(jnp.int32, (CHUNK, width), 1) & (CHUNK - 1)
    low_half = lax.broadcasted_iota(jnp.int32, (CHUNK, 128), 1) < CHUNK
    eye = (ri == cj).astype(F32)
    masks = ((ri >= cj, ri > cj, CHUNK - 1), (ri <= cj, ri < cj, 0))

    def side_by_side(parts):
        return jnp.concatenate([jnp.where(low_half, parts[0], parts[1]), jnp.where(low_half, parts[2], parts[3])],
                               axis=1)

    gram, kbd = {}, {}
    for c in chunk_ids:
        rows = slice(c * CHUNK, (c + 1) * CHUNK)
        for g in range(N_GROUPS):
            cols = slice(g * 512, (g + 1) * 512)
            k4 = k_ref[rows, cols]
            kbd[c, g] = _block_diag(k4, GROUP)
            gram[c, g] = _dot_nt(jnp.concatenate([q_ref[rows, cols], k4], axis=0), kbd[c, g])
    yield

    x, p, dq, dk = {}, {}, {}, {}
    is_eye = ri == cj
    for c in chunk_ids:
        rows = slice(c * CHUNK, (c + 1) * CHUNK)
        for d, (incl, strict, last) in enumerate(masks):
            for g in range(N_GROUPS):
                beta_cols, gc_cols, last_rows = [], [], []
                for a in range(GROUP):
                    h = g * GROUP + a
                    idx = d * B_HEADS + h
                    hcols = slice(h * 128, (h + 1) * 128)
                    beta_cols.append(jnp.broadcast_to(col_ref[rows, idx:idx + 1], (CHUNK, 128)))
                    gc_cols.append(jnp.broadcast_to(col_ref[rows, 2 * B_HEADS + idx:2 * B_HEADS + idx + 1],
                                                    (CHUNK, 128)))
                    t_last = c * CHUNK + last
                    g_last = row_ref[2 * B_HEADS + idx:2 * B_HEADS + idx + 1, t_last:t_last + 1]
                    egl_ref[d, c, :, hcols] = jnp.exp(jnp.broadcast_to(g_last, (1, 128)))
                    last_rows.append(jnp.broadcast_to(g_last, (1, CHUNK)))
                lanes = slice(g * width, (g + 1) * width)
                gc_row = row_piece(2 + d, c, g)
                gc_col = side_by_side(gc_cols)
                dq[c, d, g] = jnp.where(is_eye, jnp.exp(gc_col), 0.0).astype(BF16)
                dk[c, d, g] = jnp.where(is_eye, jnp.exp(jnp.concatenate(last_rows, axis=1) - gc_col), 0.0).astype(BF16)
                decay = jnp.where(incl, jnp.exp(jnp.where(incl, gc_col - gc_row, 0.0)), 0.0)
                gm = gram[c, g]
                qk_ref[d, rows, lanes] = (gm[:CHUNK] * decay).astype(BF16)
                low = jnp.where(strict, gm[CHUNK:] * decay * side_by_side(beta_cols), 0.0)
                x[c, d, g] = -low
                p[c, d, g] = eye - low
        if c % 2 == 1:
            yield

    keys = list(x)
    for key in keys:
        xb = x[key].astype(BF16)
        x[key] = _dot(xb, _block_diag(xb, GROUP))
    yield
    squarings = CHUNK.bit_length() - 2
    for _ in range(squarings - 1):
        for key in keys:
            xb = x[key].astype(BF16)
            z = _dot(jnp.concatenate([xb, p[key].astype(BF16)], axis=0), _block_diag(xb, GROUP))
            x[key] = z[:CHUNK]
            p[key] = p[key] + z[CHUNK:]
        yield
    for key in keys:
        p[key] = p[key] + _dot(p[key].astype(BF16), _block_diag(x[key].astype(BF16), GROUP))
    yield

    for c in chunk_ids:
        rows = slice(c * CHUNK, (c + 1) * CHUNK)
        for g in range(N_GROUPS):
            cols = slice(g * 512, (g + 1) * 512)
            tb = [p[c, d, g] * row_piece(d, c, g) for d in range(2)]
            tbg = [tb[d] * jnp.exp(row_piece(2 + d, c, g)) for d in range(2)]
            uu = _dot(jnp.concatenate(tb, axis=0).astype(BF16), _block_diag(v_ref[rows, cols], GROUP))
            k_lhs = jnp.concatenate([jnp.concatenate(tbg, axis=0).astype(BF16), dk[c, 0, g], dk[c, 1, g]], axis=0)
            ww = _dot(k_lhs, kbd[c, g])
            qq = _dot(jnp.concatenate([dq[c, 0, g], dq[c, 1, g]], axis=0), _block_diag(q_ref[rows, cols], GROUP))
            for d in range(2):
                u_ref[d, rows, cols] = uu[d * CHUNK:(d + 1) * CHUNK].astype(BF16)
                w_ref[d, rows, cols] = ww[d * CHUNK:(d + 1) * CHUNK].astype(BF16)
                kd_ref[d, rows, cols] = ww[(2 + d) * CHUNK:(3 + d) * CHUNK].astype(BF16)
                qg_ref[d, rows, cols] = qq[d * CHUNK:(d + 1) * CHUNK].astype(BF16)
    yield


SCAN_CHUNKS = 2


def _delta_scan_kernel(*refs):
    per_dir = 6
    dirs = (refs[:per_dir] + (refs[2 * per_dir],), refs[per_dir:2 * per_dir] + (refs[2 * per_dir + 1],))
    state_ref = refs[2 * per_dir + 2]
    n_batch = dirs[0][0].shape[0]

    @pl.when(pl.program_id(0) == 0)
    def _():
        state_ref[...] = jnp.zeros_like(state_ref)

    tile = PAIR * B_HEAD_K
    units = [(i, d, hp) for i in range(n_batch) for d in range(2) for hp in range(N_PAIRS)]
    slot = lambda i, d, hp: (i * 2 + d) * N_PAIRS + hp
    for step in range(SCAN_CHUNKS):
        sub = (step, SCAN_CHUNKS - 1 - step)
        rows = [slice(c * CHUNK, (c + 1) * CHUNK) for c in sub]
        rs = {}
        for i, d, hp in units:
            w_ref, _, qg_ref = dirs[d][:3]
            cols = slice(hp * tile, (hp + 1) * tile)
            lhs = jnp.concatenate([w_ref[i, rows[d], cols], qg_ref[i, rows[d], cols]], axis=0)
            rs[i, d, hp] = _dot(lhs, state_ref[slot(i, d, hp)].astype(BF16))
        for i, d, hp in units:
            _, u_ref, _, kd_ref, qk_ref, egl_ref, o_ref = dirs[d]
            cols = slice(hp * tile, (hp + 1) * tile)
            r = rs[i, d, hp]
            v_new = (u_ref[i, rows[d], cols].astype(F32) - r[:CHUNK]).astype(BF16)
            qk = qk_ref[i, rows[d], hp * PAIR * CHUNK:(hp + 1) * PAIR * CHUNK]
            o_ref[i, rows[d], cols] = (r[CHUNK:] + _dot(qk, _block_diag(v_new, PAIR))).astype(o_ref.dtype)
            idx = slot(i, d, hp)
            for a in range(PAIR):
                blk = slice(a * B_HEAD_K, (a + 1) * B_HEAD_K)
                hcols = slice(hp * tile + a * B_HEAD_K, hp * tile + (a + 1) * B_HEAD_K)
                update = _dot_tn(kd_ref[i, rows[d], hcols], v_new[:, blk])
                state_ref[idx, blk, blk] = state_ref[idx, blk, blk] * egl_ref[i, sub[d], :, hcols] + update


def _delta_scan(w, u, qg, kd, qk, egl):
    b, _, s, kw = w.shape
    rows = SCAN_CHUNKS * CHUNK
    assert s % rows == 0
    n_steps = s // rows
    chunk_of = (lambda c: c, lambda c: n_steps - 1 - c)
    in_specs = []
    for d in range(2):
        act = lambda n, d=d: pl.BlockSpec((b, None, rows, n), lambda c: (0, d, chunk_of[d](c), 0))
        in_specs += [act(kw), act(kw), act(kw), act(kw), act(B_HEADS * CHUNK),
                     pl.BlockSpec((b, None, SCAN_CHUNKS, 1, kw), lambda c, d=d: (0, d, chunk_of[d](c), 0, 0))]
    out = jax.ShapeDtypeStruct((b, s, kw), BF16)
    out_specs = [pl.BlockSpec((b, rows, kw), lambda c, d=d: (0, chunk_of[d](c), 0)) for d in range(2)]
    args = (w, u, qg, kd, qk, egl)
    return pl.pallas_call(
        _delta_scan_kernel,
        grid=(n_steps,),
        in_specs=in_specs,
        out_specs=out_specs,
        out_shape=[out, out],
        scratch_shapes=[pltpu.VMEM((b * 2 * N_PAIRS, PAIR * B_HEAD_K, PAIR * B_HEAD_V), F32)],
        compiler_params=_params(("arbitrary",)),
        name="delta_scan",
    )(*args, *args)


def _delta_out_kernel(of_ref, ob_ref, gate_ref, x_ref, onorm_ref, w_ref, fnorm_ref, o_ref, og_scr):
    onorm = onorm_ref[...]
    for h in range(B_HEADS):
        cols = slice(h * 128, (h + 1) * 128)
        o = of_ref[:, cols].astype(F32) + ob_ref[:, cols].astype(F32)
        o = o * lax.rsqrt(jnp.mean(o * o, axis=-1, keepdims=True) + EPS) * onorm
        gate = gate_ref[:, cols].astype(F32)
        og_scr[:, cols] = (o * _silu(gate)).astype(BF16)
    y = x_ref[...] + _dot(og_scr[...], w_ref[...])
    o_ref[...] = _rms_rows(y, fnorm_ref[...])


def _delta_out(o_f, o_b, gate, x1, out_norm, w_out, final_norm):
    t = x1.shape[0]
    tm = 512
    assert t % tm == 0
    row = pl.BlockSpec((tm, D_MODEL), lambda i: (i, 0))
    vec = lambda n: pl.BlockSpec((1, n), lambda i: (0, 0))
    return pl.pallas_call(
        _delta_out_kernel,
        grid=(t // tm,),
        in_specs=[row, row, row, row, vec(B_HEAD_V), pl.BlockSpec(w_out.shape, lambda i: (0, 0)), vec(D_MODEL)],
        out_specs=row,
        out_shape=jax.ShapeDtypeStruct((t, D_MODEL), F32),
        scratch_shapes=[pltpu.VMEM((tm, D_MODEL), BF16)],
        compiler_params=_params(("parallel",)),
        name="delta_out",
    )(o_f, o_b, gate, x1, out_norm, w_out, final_norm)


def kernel(x, attn_norm, attn_w_in, attn_sink, attn_w_out, delta_norm, delta_w_in, delta_conv, delta_a_log,
           delta_dt_bias, delta_out_norm, delta_w_out, final_norm):
    b, s, d = x.shape
    assert d == D_MODEL
    t = b * s
    q_t, k, v_t, gate_t = _attn_proj(x, attn_norm.reshape(1, d).astype(F32), attn_w_in)
    x1 = _attn_core(x, q_t, k, v_t, gate_t, attn_sink.astype(F32), attn_w_out.astype(BF16))
    dgate, *operators = _delta_front(x1, delta_norm.reshape(1, d).astype(F32), delta_w_in, delta_conv,
                                     delta_a_log, delta_dt_bias)
    o_f, o_b = _delta_scan(*operators)
    out = _delta_out(o_f.reshape(t, d), o_b.reshape(t, d), dgate.reshape(t, d), x1.reshape(t, d),
                     delta_out_norm.reshape(1, B_HEAD_V).astype(F32), delta_w_out.astype(BF16),
                     final_norm.reshape(1, d).astype(F32))
    return out.reshape(b, s, d)
```

```python
import functools
import itertools

import numpy as np
import jax
import jax.numpy as jnp
from jax import lax
from jax.experimental import pallas as pl
from jax.experimental.pallas import tpu as pltpu

F32 = jnp.float32
BF16 = jnp.bfloat16
EPS = 1e-6
LOG2E = float(np.log2(np.e))

D_MODEL = 1024
A_HEADS = 16
A_KV_HEADS = 4
A_HEAD_DIM = 64
A_PAIRS = A_HEADS // 2
BLOCK = 128
ATTN_TILE = 512
KEYS = 3 * BLOCK

B_HEADS = 8
B_HEAD_K = 128
B_HEAD_V = 128
CONV_WIDTH = 5
CONV_PAD = CONV_WIDTH // 2
CHUNK = 64
N_QKV = 3 * B_HEADS * B_HEAD_K
HALO = 8
PROJ_BLOCK = 256
PROJ_RING = 3

V7X_VMEM_LIMIT_BYTES = 56 * 1024 * 1024


def _sigmoid(x):
    return 1.0 / (1.0 + jnp.exp(-x))


def _silu(x):
    half = 0.5 * x
    return half + half * jnp.tanh(half)


def _softplus(x):
    return jnp.maximum(x, 0.0) + jnp.log(1.0 + jnp.exp(-jnp.abs(x)))


def _rms_rows(x, gain):
    ms = jnp.mean(x * x, axis=-1, keepdims=True)
    return x * lax.rsqrt(ms + EPS) * gain


def _split3(x):
    a = x.astype(BF16)
    r = x - a.astype(F32)
    b = r.astype(BF16)
    c = (r - b.astype(F32)).astype(BF16)
    return a, b, c


def _dot(a, b):
    return jnp.dot(a, b, preferred_element_type=F32)


def _dot_nt(a, b):
    return lax.dot_general(a, b, (((1,), (1,)), ((), ())), preferred_element_type=F32)


def _dot_tn(a, b):
    return lax.dot_general(a, b, (((0,), (0,)), ((), ())), preferred_element_type=F32)


def _params(semantics):
    return pltpu.CompilerParams(dimension_semantics=semantics, vmem_limit_bytes=V7X_VMEM_LIMIT_BYTES)


def _attn_proj_kernel(x_ref, gain_ref, wqt_ref, wk_ref, wvt_ref, wgt_ref, qt_ref, k_ref, vt_ref, gt_ref):
    hn = _rms_rows(x_ref[...], gain_ref[...]).astype(BF16)
    qt_ref[...] = (_dot_nt(wqt_ref[...], hn) * (A_HEAD_DIM ** -0.5 * LOG2E)).astype(BF16)
    k_ref[...] = _dot(hn, wk_ref[...]).astype(BF16)
    vt_ref[...] = _dot_nt(wvt_ref[...], hn).astype(BF16)
    gt_ref[...] = _dot_nt(wgt_ref[...], hn).astype(BF16)


def _attn_proj(x, gain, w_in):
    b, s, _ = x.shape
    tm = ATTN_TILE
    assert s % tm == 0
    kvw = A_KV_HEADS * A_HEAD_DIM
    wqt = w_in[:, :D_MODEL].T.astype(BF16)
    wk = w_in[:, D_MODEL:D_MODEL + kvw].reshape(D_MODEL, A_KV_HEADS, A_HEAD_DIM)
    wk = jnp.concatenate([wk, jnp.zeros_like(wk)], axis=2).reshape(D_MODEL, 2 * kvw).astype(BF16)
    wvt = w_in[:, D_MODEL + kvw:D_MODEL + 2 * kvw].T.astype(BF16)
    wgt = w_in[:, D_MODEL + 2 * kvw:].T.astype(BF16)
    full = lambda a: pl.BlockSpec(a.shape, lambda i, j: (0, 0), pipeline_mode=pl.Buffered(1))
    feat = lambda n: pl.BlockSpec((None, None, n, tm), lambda i, j: (i, j, 0, 0))
    feat_shape = lambda n: jax.ShapeDtypeStruct((b, s // tm, n, tm), BF16)
    return pl.pallas_call(
        _attn_proj_kernel,
        grid=(b, s // tm),
        in_specs=[pl.BlockSpec((None, tm, D_MODEL), lambda i, j: (i, j, 0)), full(gain),
                  full(wqt), full(wk), full(wvt), full(wgt)],
        out_specs=[feat(D_MODEL), pl.BlockSpec((None, tm, 2 * kvw), lambda i, j: (i, j, 0)), feat(kvw), feat(D_MODEL)],
        out_shape=[feat_shape(D_MODEL), jax.ShapeDtypeStruct((b, s, 2 * kvw), BF16), feat_shape(kvw),
                   feat_shape(D_MODEL)],
        compiler_params=_params(("parallel", "parallel")),
        name="attn_proj",
    )(x, gain, wqt, wk, wvt, wgt)


A_GROUP = A_HEADS // A_KV_HEADS
ONES_ROWS = 16


def _attn_core_kernel(sink_ref, qt_ref, kp_ref, kc_ref, kn_ref, vp_ref, vc_ref, vn_ref, gt_ref, x_ref, bias_ref,
                      w_ref, o_ref, k_scr, vt_scr, *, tq, n_tiles):
    j = pl.program_id(1)
    k_scr[0:BLOCK, :] = kp_ref[...]
    k_scr[BLOCK:BLOCK + tq, :] = kc_ref[...]
    k_scr[BLOCK + tq:, :] = kn_ref[...]
    vt_scr[:, 0:BLOCK] = vp_ref[...]
    vt_scr[:, BLOCK:BLOCK + tq] = vc_ref[...]
    vt_scr[:, BLOCK + tq:] = vn_ref[...]
    zero_rows = jnp.zeros((A_HEAD_DIM, BLOCK), BF16)
    ones_rows = jnp.ones((ONES_ROWS, KEYS), BF16)
    n_qb = tq // BLOCK

    def scores(qb, g):
        keys = slice(qb * BLOCK, qb * BLOCK + KEYS)
        qcols = slice(qb * BLOCK, (qb + 1) * BLOCK)
        q_t = jnp.concatenate(
            [jnp.concatenate([qt_ref[(A_GROUP * g + r) * A_HEAD_DIM:(A_GROUP * g + r + 1) * A_HEAD_DIM, qcols],
                              zero_rows], axis=0) for r in range(A_GROUP)], axis=1)
        gqb = j * n_qb + qb
        table = jnp.where(gqb == 0, 1, jnp.where(gqb == n_tiles * n_qb - 1, 2, 0)) * A_KV_HEADS
        return _dot(k_scr[keys, g * 128:(g + 1) * 128], q_t) + bias_ref[table + g]

    def out_chunk(qb, og_t, c):
        rows = slice(qb * BLOCK, (qb + 1) * BLOCK)
        cols = slice(c * 256, (c + 1) * 256)
        o_ref[rows, cols] = x_ref[rows, cols] + _dot_tn(og_t, w_ref[:, cols])

    items = [(qb, g) for qb in range(n_qb) for g in range(A_KV_HEADS)]
    pending = []
    og_parts = []
    s_next = scores(*items[0])
    for n, (qb, g) in enumerate(items):
        keys = slice(qb * BLOCK, qb * BLOCK + KEYS)
        qcols = slice(qb * BLOCK, (qb + 1) * BLOCK)
        s = s_next
        if n + 1 < len(items):
            s_next = scores(*items[n + 1])
        if pending:
            pending.pop(0)()
        sink = sink_ref[g]
        m = jnp.maximum(jnp.max(s, axis=0, keepdims=True), sink)
        probs = jnp.exp2(s - m).astype(BF16)
        v_t = jnp.concatenate([vt_scr[g * A_HEAD_DIM:(g + 1) * A_HEAD_DIM, keys], ones_rows], axis=0)
        o_ext = _dot(v_t, probs)
        inv = 1.0 / (o_ext[A_HEAD_DIM:A_HEAD_DIM + 1] + jnp.exp2(sink - m))
        for r in range(A_GROUP):
            h = A_GROUP * g + r
            lanes = slice(r * BLOCK, (r + 1) * BLOCK)
            gate = gt_ref[h * A_HEAD_DIM:(h + 1) * A_HEAD_DIM, qcols].astype(F32)
            og_parts.append((o_ext[:A_HEAD_DIM, lanes] * inv[:, lanes] * _silu(gate)).astype(BF16))
        if g == A_KV_HEADS - 1:
            og_t = jnp.concatenate(og_parts, axis=0)
            og_parts = []
            pending += [functools.partial(out_chunk, qb, og_t, c) for c in range(D_MODEL // 256)]
    for piece in pending:
        piece()


def _attn_tables():
    qpos = np.arange(BLOCK)[:, None]
    krel = np.arange(KEYS)[None, :] - BLOCK
    dist = np.abs(krel - qpos).astype(np.float32)
    slopes = np.power(2.0, -8.0 * np.arange(1, A_HEADS + 1) / A_HEADS).astype(np.float32)
    band = np.where(dist <= BLOCK, 0.0, -np.inf)
    per_head = (-slopes[:, None, None] * dist[None]).astype(np.float64) * LOG2E + band[None]
    sets = np.broadcast_to(per_head[None], (3, A_HEADS, BLOCK, KEYS)).copy()
    sets[1, :, :, :BLOCK] = -np.inf
    sets[2, :, :, 2 * BLOCK:] = -np.inf
    sets = sets.reshape(3, A_KV_HEADS, A_GROUP, BLOCK, KEYS).transpose(0, 1, 4, 2, 3)
    return jnp.asarray(sets.reshape(3 * A_KV_HEADS, KEYS, A_GROUP * BLOCK), F32)


def _attn_core(x, q_t, k, v_t, gate_t, sink, w_out):
    b, s, _ = x.shape
    tq = ATTN_TILE
    assert s % tq == 0 and s // BLOCK >= 2
    n_tiles = s // tq
    nb = s // BLOCK
    per = tq // BLOCK
    kw = k.shape[-1]
    vw = v_t.shape[2]
    bias = _attn_tables()
    sink_rows = jnp.repeat((sink * LOG2E).reshape(A_KV_HEADS, 1, A_GROUP), BLOCK, axis=2)
    tile = pl.BlockSpec((None, tq, D_MODEL), lambda i, j: (i, j, 0))
    feat = lambda n: pl.BlockSpec((None, None, n, tq), lambda i, j: (i, j, 0, 0))
    prev = lambda j: jnp.maximum(j * per - 1, 0)
    nxt = lambda j: jnp.minimum((j + 1) * per, nb - 1)
    const = lambda a: pl.BlockSpec(a.shape, lambda i, j: (0,) * a.ndim, pipeline_mode=pl.Buffered(1))
    kernel = functools.partial(_attn_core_kernel, tq=tq, n_tiles=n_tiles)
    return pl.pallas_call(
        kernel,
        grid=(b, n_tiles),
        in_specs=[
            const(sink_rows),
            feat(D_MODEL),
            pl.BlockSpec((None, BLOCK, kw), lambda i, j: (i, prev(j), 0)),
            pl.BlockSpec((None, tq, kw), lambda i, j: (i, j, 0)),
            pl.BlockSpec((None, BLOCK, kw), lambda i, j: (i, nxt(j), 0)),
            pl.BlockSpec((None, None, vw, BLOCK), lambda i, j: (i, jnp.maximum(j - 1, 0), 0, per - 1)),
            feat(vw),
            pl.BlockSpec((None, None, vw, BLOCK), lambda i, j: (i, jnp.minimum(j + 1, n_tiles - 1), 0, 0)),
            feat(D_MODEL),
            tile,
            const(bias),
            const(w_out),
        ],
        out_specs=tile,
        out_shape=jax.ShapeDtypeStruct(x.shape, F32),
        scratch_shapes=[pltpu.VMEM((tq + 2 * BLOCK, kw), BF16), pltpu.VMEM((vw, tq + 2 * BLOCK), BF16)],
        compiler_params=_params(("parallel", "parallel")),
        name="attn_core",
    )(sink_rows, q_t, k, k, k, v_t, v_t, v_t, gate_t, x, bias, w_out)


def _proj_stages(j, xp_ref, xc_ref, xn_ref, gain_ref, wqkv_ref, wg_ref, wba_ref, wbat_ref, conv_ref,
                 alog_ref, dtb_ref, alogc_ref, dtbc_ref, tril_ref, triu_ref, trilt_ref, triut_ref,
                 q_ref, k_ref, v_ref, gate_ref, col_ref, row_ref, hn_scr, proj_scr, *, tm, n_tiles):
    gain = gain_ref[...]
    keep_prev = (j > 0).astype(F32)
    keep_next = (j < n_tiles - 1).astype(F32)
    hn_scr[0:HALO, :] = (_rms_rows(xp_ref[...], gain) * keep_prev).astype(BF16)
    hn_scr[HALO:HALO + tm, :] = _rms_rows(xc_ref[...], gain).astype(BF16)
    hn_scr[HALO + tm:, :] = (_rms_rows(xn_ref[...], gain) * keep_next).astype(BF16)
    hn = hn_scr[HALO:HALO + tm, :]

    def gate_and_terms():
        gate_ref[...] = _dot(hn, wg_ref[...]).astype(BF16)
        ba = _dot(hn, wba_ref[...])
        g_c = -jnp.exp(alog_ref[...]) * _softplus(ba + dtb_ref[...])
        bat = _dot_nt(wbat_ref[...], hn)
        g_r = -jnp.exp(alogc_ref[...]) * _softplus(bat + dtbc_ref[...])
        return _sigmoid(ba), _sigmoid(bat), _split3(g_c), _split3(g_r)

    def decay_sums(beta_c, beta_r, gc3, gr3):
        ent_r = lax.broadcasted_iota(jnp.int32, (4 * B_HEADS, tm), 0)
        cum_r = jnp.where(ent_r < 3 * B_HEADS, sum(_dot(p, trilt_ref[...]) for p in gr3),
                          sum(_dot(p, triut_ref[...]) for p in gr3))
        row_ref[...] = jnp.where(ent_r < 2 * B_HEADS, beta_r, cum_r)
        ent_c = lax.broadcasted_iota(jnp.int32, (CHUNK, 4 * B_HEADS), 1)
        for c in range(tm // CHUNK):
            rows = slice(c * CHUNK, (c + 1) * CHUNK)
            fwd = sum(_dot(tril_ref[...], p[rows]) for p in gc3)
            bwd = sum(_dot(triu_ref[...], p[rows]) for p in gc3)
            col_ref[rows, :] = jnp.where(ent_c < 2 * B_HEADS, beta_c[rows], jnp.where(ent_c < 3 * B_HEADS, fwd, bwd))

    taps = conv_ref[...]
    hn_ext = hn_scr[...]
    n_blocks = N_QKV // PROJ_BLOCK
    rc = 128

    def project(n):
        proj_scr[n % PROJ_RING] = _dot(hn_ext, wqkv_ref[:, n * PROJ_BLOCK:(n + 1) * PROJ_BLOCK])

    def conv(n):
        for half in range(PROJ_BLOCK // 128):
            cb = n * (PROJ_BLOCK // 128) + half
            cols = slice(cb * 128, (cb + 1) * 128)
            out_ref, head = ((q_ref, k_ref, v_ref)[cb // B_HEADS], cb % B_HEADS)
            for r0 in range(0, tm, rc):
                window = proj_scr[n % PROJ_RING, r0:r0 + rc + 2 * HALO, half * 128:(half + 1) * 128]
                acc = None
                for t in range(CONV_WIDTH):
                    shift = CONV_PAD - t
                    moved = window if shift == 0 else pltpu.roll(window, shift % (rc + 2 * HALO), axis=0)
                    term = taps[t:t + 1, cols] * moved[HALO:HALO + rc]
                    acc = term if acc is None else acc + term
                y = _silu(acc)
                if out_ref is not v_ref:
                    scale = 1.0 if out_ref is k_ref else B_HEAD_K ** -0.5
                    y = y * (lax.rsqrt(jnp.sum(y * y, axis=-1, keepdims=True) + EPS) * scale)
                out_ref[r0:r0 + rc, head * 128:(head + 1) * 128] = y.astype(BF16)

    project(0)
    terms = gate_and_terms()
    yield
    for n in range(n_blocks):
        if n + 1 < n_blocks:
            project(n + 1)
        yield
        conv(n)
        if n == 1:
            decay_sums(*terms)


def _delta_front_kernel(*refs, tm, n_tiles, n_total):
    n_in, n_out = 17, 7
    ins, outs, scr = refs[:n_in], refs[n_in:n_in + n_out], refs[n_in + n_out:]
    gate_ref, w_ref, u_ref, qg_ref, kd_ref, qk_ref, egl_ref = outs
    hn_scr, proj_scr, q_scr, k_scr, v_scr, col_scr, row_scr = scr
    j = pl.program_id(0)

    @pl.when(j == 0)
    def _():
        for ref in (q_scr, k_scr, v_scr, col_scr, row_scr):
            ref[1] = jnp.zeros(ref.shape[1:], ref.dtype)

    cur = j % 2
    prev = 1 - cur
    tile = lax.rem(jnp.minimum(j, n_total - 1), n_tiles)
    proj = _proj_stages(tile, *ins, q_scr.at[cur], k_scr.at[cur], v_scr.at[cur], gate_ref, col_scr.at[cur],
                        row_scr.at[cur], hn_scr, proj_scr, tm=tm, n_tiles=n_tiles)
    n_chunks = tm // CHUNK
    waves = [range(c0, c0 + PREP_WAVE) for c0 in range(0, n_chunks, PREP_WAVE)]
    prep = itertools.chain.from_iterable(
        _prep_stages(q_scr.at[prev], k_scr.at[prev], v_scr.at[prev], col_scr.at[prev], row_scr.at[prev],
                     w_ref, u_ref, qg_ref, kd_ref, qk_ref, egl_ref, chunk_ids=wave) for wave in waves)
    live = {"proj": proj, "prep": prep}
    turn = 0
    while live:
        for name in ("proj", "prep", "prep"):
            if name in live and next(live[name], StopIteration) is StopIteration:
                del live[name]
        turn += 1


def _delta_front(x1, gain, w_in, conv_w, a_log, dt_bias):
    b, s, _ = x1.shape
    tm = FRONT_TILE
    assert s % tm == 0
    n_tiles = s // tm
    nc = s // CHUNK
    per = tm // HALO
    nh = s // HALO
    kw = B_HEADS * B_HEAD_K
    wqkv = w_in[:, :N_QKV].astype(BF16)
    wg = w_in[:, N_QKV:N_QKV + kw].astype(BF16)
    wba = w_in[:, N_QKV + kw:].astype(BF16)
    wbat = wba.T
    pad = jnp.zeros((1, 2 * B_HEADS), F32)
    alog = jnp.concatenate([pad, a_log.reshape(1, 2 * B_HEADS).astype(F32)], axis=1)
    dtb = jnp.concatenate([pad, dt_bias.reshape(1, 2 * B_HEADS).astype(F32)], axis=1)
    i = np.arange(CHUNK)
    tril = (i[:, None] >= i[None, :]).astype(np.float32)
    triu = tril.T
    eye = np.eye(tm // CHUNK, dtype=np.float32)
    trilt = jnp.asarray(np.kron(eye, tril.T), BF16)
    triut = jnp.asarray(np.kron(eye, triu.T), BF16)
    tril = jnp.asarray(tril, BF16)
    triu = jnp.asarray(triu, BF16)
    full = lambda a: pl.BlockSpec(a.shape, lambda j_: (0,) * a.ndim, pipeline_mode=pl.Buffered(1))
    n_total = b * n_tiles
    cur = lambda j_: divmod(jnp.minimum(j_, n_total - 1), n_tiles)
    done = lambda j_: divmod(jnp.maximum(j_ - 1, 0), n_tiles)
    both = lambda n: pl.BlockSpec((None, 2, tm, n), lambda j_: (done(j_)[0], 0, done(j_)[1], 0))
    act = jax.ShapeDtypeStruct((b, 2, s, kw), BF16)
    kernel = functools.partial(_delta_front_kernel, tm=tm, n_tiles=n_tiles, n_total=n_total)
    consts = [gain, wqkv, wg, wba, wbat, conv_w.astype(F32), alog, dtb, alog.T, dtb.T, tril, triu, trilt, triut]
    return pl.pallas_call(
        kernel,
        grid=(n_total + 1,),
        in_specs=[
            pl.BlockSpec((None, HALO, D_MODEL), lambda j_: (cur(j_)[0], jnp.maximum(cur(j_)[1] * per - 1, 0), 0)),
            pl.BlockSpec((None, tm, D_MODEL), lambda j_: (cur(j_)[0], cur(j_)[1], 0)),
            pl.BlockSpec((None, HALO, D_MODEL),
                         lambda j_: (cur(j_)[0], jnp.minimum((cur(j_)[1] + 1) * per, nh - 1), 0)),
        ] + [full(a) for a in consts],
        out_specs=[pl.BlockSpec((None, tm, kw), lambda j_: (cur(j_)[0], cur(j_)[1], 0)),
                   both(kw), both(kw), both(kw), both(kw), both(B_HEADS * CHUNK),
                   pl.BlockSpec((None, 2, tm // CHUNK, 1, kw), lambda j_: (done(j_)[0], 0, done(j_)[1], 0, 0))],
        out_shape=[jax.ShapeDtypeStruct((b, s, kw), BF16), act, act, act, act,
                   jax.ShapeDtypeStruct((b, 2, s, B_HEADS * CHUNK), BF16),
                   jax.ShapeDtypeStruct((b, 2, nc, 1, kw), F32)],
        scratch_shapes=[pltpu.VMEM((tm + 2 * HALO, D_MODEL), BF16),
                        pltpu.VMEM((PROJ_RING, tm + 2 * HALO, PROJ_BLOCK), F32),
                        pltpu.VMEM((2, tm, kw), BF16), pltpu.VMEM((2, tm, kw), BF16), pltpu.VMEM((2, tm, kw), BF16),
                        pltpu.VMEM((2, tm, 4 * B_HEADS), F32), pltpu.VMEM((2, 4 * B_HEADS, tm), F32)],
        compiler_params=_params(("arbitrary",)),
        name="delta_front",
    )(x1, x1, x1, *consts)


FRONT_TILE = 256
PREP_WAVE = 4
GROUP = 4
N_GROUPS = B_HEADS // GROUP
PAIR = 2
N_PAIRS = B_HEADS // PAIR


def _block_diag(x, n_blocks):
    r, width = x.shape
    shift = (width // n_blocks).bit_length() - 1
    blk = lax.broadcasted_iota(jnp.int32, (r, width), 1) >> shift
    zero = jnp.zeros_like(x)
    return jnp.concatenate([jnp.where(blk == a, x, zero) for a in range(n_blocks)], axis=0)


def _prep_stages(q_ref, k_ref, v_ref, col_ref, row_ref, w_ref, u_ref, qg_ref, kd_ref, qk_ref, egl_ref, *, chunk_ids):
    width = GROUP * CHUNK

    def row_piece(kind, c, g):
        first = kind * B_HEADS + g * GROUP
        return jnp.concatenate([row_ref[first + a:first + a + 1, c * CHUNK:(c + 1) * CHUNK] for a in range(GROUP)],
                               axis=1)

    ri = lax.broadcasted_iota(jnp.int32, (CHUNK, width), 0)
    cj = lax.broadcasted_iota(jnp.int32, (CHUNK, width), 1) & (CHUNK - 1)
    low_half = lax.broadcasted_iota(jnp.int32, (CHUNK, 128), 1) < CHUNK
    eye = (ri == cj).astype(F32)
    masks = ((ri >= cj, ri > cj, CHUNK - 1), (ri <= cj, ri < cj, 0))

    def side_by_side(parts):
        return jnp.concatenate([jnp.where(low_half, parts[0], parts[1]), jnp.where(low_half, parts[2], parts[3])],
                               axis=1)

    gram, kbd = {}, {}
    for c in chunk_ids:
        rows = slice(c * CHUNK, (c + 1) * CHUNK)
        for g in range(N_GROUPS):
            cols = slice(g * 512, (g + 1) * 512)
            k4 = k_ref[rows, cols]
            kbd[c, g] = _block_diag(k4, GROUP)
            gram[c, g] = _dot_nt(jnp.concatenate([q_ref[rows, cols], k4], axis=0), kbd[c, g])
    yield

    x, p, dq, dk = {}, {}, {}, {}
    is_eye = ri == cj
    for c in chunk_ids:
        rows = slice(c * CHUNK, (c + 1) * CHUNK)
        for d, (incl, strict, last) in enumerate(masks):
            for g in range(N_GROUPS):
                beta_cols, gc_cols, last_rows = [], [], []
                for a in range(GROUP):
                    h = g * GROUP + a
                    idx = d * B_HEADS + h
                    hcols = slice(h * 128, (h + 1) * 128)
                    beta_cols.append(jnp.broadcast_to(col_ref[rows, idx:idx + 1], (CHUNK, 128)))
                    gc_cols.append(jnp.broadcast_to(col_ref[rows, 2 * B_HEADS + idx:2 * B_HEADS + idx + 1],
                                                    (CHUNK, 128)))
                    t_last = c * CHUNK + last
                    g_last = row_ref[2 * B_HEADS + idx:2 * B_HEADS + idx + 1, t_last:t_last + 1]
                    egl_ref[d, c, :, hcols] = jnp.exp(jnp.broadcast_to(g_last, (1, 128)))
                    last_rows.append(jnp.broadcast_to(g_last, (1, CHUNK)))
                lanes = slice(g * width, (g + 1) * width)
                gc_row = row_piece(2 + d, c, g)
                gc_col = side_by_side(gc_cols)
                dq[c, d, g] = jnp.where(is_eye, jnp.exp(gc_col), 0.0).astype(BF16)
                dk[c, d, g] = jnp.where(is_eye, jnp.exp(jnp.concatenate(last_rows, axis=1) - gc_col), 0.0).astype(BF16)
                decay = jnp.where(incl, jnp.exp(jnp.where(incl, gc_col - gc_row, 0.0)), 0.0)
                gm = gram[c, g]
                qk_ref[d, rows, lanes] = (gm[:CHUNK] * decay).astype(BF16)
                low = jnp.where(strict, gm[CHUNK:] * decay * side_by_side(beta_cols), 0.0)
                x[c, d, g] = -low
                p[c, d, g] = eye - low
        if c % 2 == 1:
            yield

    keys = list(x)
    for key in keys:
        xb = x[key].astype(BF16)
        x[key] = _dot(xb, _block_diag(xb, GROUP))
    yield
    squarings = CHUNK.bit_length() - 2
    for _ in range(squarings - 1):
        for key in keys:
            xb = x[key].astype(BF16)
            z = _dot(jnp.concatenate([xb, p[key].astype(BF16)], axis=0), _block_diag(xb, GROUP))
            x[key] = z[:CHUNK]
            p[key] = p[key] + z[CHUNK:]
        yield
    for key in keys:
        p[key] = p[key] + _dot(p[key].astype(BF16), _block_diag(x[key].astype(BF16), GROUP))
    yield

    for c in chunk_ids:
        rows = slice(c * CHUNK, (c + 1) * CHUNK)
        for g in range(N_GROUPS):
            cols = slice(g * 512, (g + 1) * 512)
            tb = [p[c, d, g] * row_piece(d, c, g) for d in range(2)]
            tbg = [tb[d] * jnp.exp(row_piece(2 + d, c, g)) for d in range(2)]
            uu = _dot(jnp.concatenate(tb, axis=0).astype(BF16), _block_diag(v_ref[rows, cols], GROUP))
            k_lhs = jnp.concatenate([jnp.concatenate(tbg, axis=0).astype(BF16), dk[c, 0, g], dk[c, 1, g]], axis=0)
            ww = _dot(k_lhs, kbd[c, g])
            qq = _dot(jnp.concatenate([dq[c, 0, g], dq[c, 1, g]], axis=0), _block_diag(q_ref[rows, cols], GROUP))
            for d in range(2):
                u_ref[d, rows, cols] = uu[d * CHUNK:(d + 1) * CHUNK].astype(BF16)
                w_ref[d, rows, cols] = ww[d * CHUNK:(d + 1) * CHUNK].astype(BF16)
                kd_ref[d, rows, cols] = ww[(2 + d) * CHUNK:(3 + d) * CHUNK].astype(BF16)
                qg_ref[d, rows, cols] = qq[d * CHUNK:(d + 1) * CHUNK].astype(BF16)
    yield


SCAN_CHUNKS = 2


def _delta_scan_kernel(*refs):
    per_dir = 6
    dirs = (refs[:per_dir] + (refs[2 * per_dir],), refs[per_dir:2 * per_dir] + (refs[2 * per_dir + 1],))
    state_ref = refs[2 * per_dir + 2]
    n_batch = dirs[0][0].shape[0]

    @pl.when(pl.program_id(0) == 0)
    def _():
        state_ref[...] = jnp.zeros_like(state_ref)

    tile = PAIR * B_HEAD_K
    units = [(i, d, hp) for i in range(n_batch) for d in range(2) for hp in range(N_PAIRS)]
    slot = lambda i, d, hp: (i * 2 + d) * N_PAIRS + hp
    for step in range(SCAN_CHUNKS):
        sub = (step, SCAN_CHUNKS - 1 - step)
        rows = [slice(c * CHUNK, (c + 1) * CHUNK) for c in sub]
        rs = {}
        for i, d, hp in units:
            w_ref, _, qg_ref = dirs[d][:3]
            cols = slice(hp * tile, (hp + 1) * tile)
            lhs = jnp.concatenate([w_ref[i, rows[d], cols], qg_ref[i, rows[d], cols]], axis=0)
            rs[i, d, hp] = _dot(lhs, state_ref[slot(i, d, hp)].astype(BF16))
        for i, d, hp in units:
            _, u_ref, _, kd_ref, qk_ref, egl_ref, o_ref = dirs[d]
            cols = slice(hp * tile, (hp + 1) * tile)
            r = rs[i, d, hp]
            v_new = (u_ref[i, rows[d], cols].astype(F32) - r[:CHUNK]).astype(BF16)
            qk = qk_ref[i, rows[d], hp * PAIR * CHUNK:(hp + 1) * PAIR * CHUNK]
            o_ref[i, rows[d], cols] = (r[CHUNK:] + _dot(qk, _block_diag(v_new, PAIR))).astype(o_ref.dtype)
            idx = slot(i, d, hp)
            for a in range(PAIR):
                blk = slice(a * B_HEAD_K, (a + 1) * B_HEAD_K)
                hcols = slice(hp * tile + a * B_HEAD_K, hp * tile + (a + 1) * B_HEAD_K)
                update = _dot_tn(kd_ref[i, rows[d], hcols], v_new[:, blk])
                state_ref[idx, blk, blk] = state_ref[idx, blk, blk] * egl_ref[i, sub[d], :, hcols] + update


def _delta_scan(w, u, qg, kd, qk, egl):
    b, _, s, kw = w.shape
    rows = SCAN_CHUNKS * CHUNK
    assert s % rows == 0
    n_steps = s // rows
    chunk_of = (lambda c: c, lambda c: n_steps - 1 - c)
    in_specs = []
    for d in range(2):
        act = lambda n, d=d: pl.BlockSpec((b, None, rows, n), lambda c: (0, d, chunk_of[d](c), 0))
        in_specs += [act(kw), act(kw), act(kw), act(kw), act(B_HEADS * CHUNK),
                     pl.BlockSpec((b, None, SCAN_CHUNKS, 1, kw), lambda c, d=d: (0, d, chunk_of[d](c), 0, 0))]
    out = jax.ShapeDtypeStruct((b, s, kw), BF16)
    out_specs = [pl.BlockSpec((b, rows, kw), lambda c, d=d: (0, chunk_of[d](c), 0)) for d in range(2)]
    args = (w, u, qg, kd, qk, egl)
    return pl.pallas_call(
        _delta_scan_kernel,
        grid=(n_steps,),
        in_specs=in_specs,
        out_specs=out_specs,
        out_shape=[out, out],
        scratch_shapes=[pltpu.VMEM((b * 2 * N_PAIRS, PAIR * B_HEAD_K, PAIR * B_HEAD_V), F32)],
        compiler_params=_params(("arbitrary",)),
        name="delta_scan",
    )(*args, *args)


OUT_TILE = 1024


def _delta_out_kernel(of_ref, ob_ref, gate_ref, x_ref, onorm_ref, w_ref, fnorm_ref, o_ref, og_scr):
    onorm = onorm_ref[...]
    for h in range(B_HEADS):
        cols = slice(h * 128, (h + 1) * 128)
        o = of_ref[:, cols].astype(F32) + ob_ref[:, cols].astype(F32)
        o = o * lax.rsqrt(jnp.mean(o * o, axis=-1, keepdims=True) + EPS) * onorm
        gate = gate_ref[:, cols].astype(F32)
        og_scr[:, cols] = (o * _silu(gate)).astype(BF16)
    y = x_ref[...] + _dot(og_scr[...], w_ref[...])
    o_ref[...] = _rms_rows(y, fnorm_ref[...])


def _delta_out(o_f, o_b, gate, x1, out_norm, w_out, final_norm):
    t = x1.shape[0]
    tm = OUT_TILE
    assert t % tm == 0
    row = pl.BlockSpec((tm, D_MODEL), lambda i: (i, 0))
    vec = lambda n: pl.BlockSpec((1, n), lambda i: (0, 0))
    return pl.pallas_call(
        _delta_out_kernel,
        grid=(t // tm,),
        in_specs=[row, row, row, row, vec(B_HEAD_V), pl.BlockSpec(w_out.shape, lambda i: (0, 0)), vec(D_MODEL)],
        out_specs=row,
        out_shape=jax.ShapeDtypeStruct((t, D_MODEL), F32),
        scratch_shapes=[pltpu.VMEM((tm, D_MODEL), BF16)],
        compiler_params=_params(("parallel",)),
        name="delta_out",
    )(o_f, o_b, gate, x1, out_norm, w_out, final_norm)


def kernel(x, attn_norm, attn_w_in, attn_sink, attn_w_out, delta_norm, delta_w_in, delta_conv, delta_a_log,
           delta_dt_bias, delta_out_norm, delta_w_out, final_norm):
    b, s, d = x.shape
    assert d == D_MODEL
    t = b * s
    q_t, k, v_t, gate_t = _attn_proj(x, attn_norm.reshape(1, d).astype(F32), attn_w_in)
    x1 = _attn_core(x, q_t, k, v_t, gate_t, attn_sink.astype(F32), attn_w_out.astype(BF16))
    dgate, *operators = _delta_front(x1, delta_norm.reshape(1, d).astype(F32), delta_w_in, delta_conv,
                                     delta_a_log, delta_dt_bias)
    o_f, o_b = _delta_scan(*operators)
    out = _delta_out(o_f.reshape(t, d), o_b.reshape(t, d), dgate.reshape(t, d), x1.reshape(t, d),
                     delta_out_norm.reshape(1, B_HEAD_V).astype(F32), delta_w_out.astype(BF16),
                     final_norm.reshape(1, d).astype(F32))
    return out.reshape(b, s, d)
```

```python
import functools
import itertools

import numpy as np
import jax
import jax.numpy as jnp
from jax import lax
from jax.experimental import pallas as pl
from jax.experimental.pallas import tpu as pltpu

F32 = jnp.float32
BF16 = jnp.bfloat16
EPS = 1e-6
LOG2E = float(np.log2(np.e))

D_MODEL = 1024
A_HEADS = 16
A_KV_HEADS = 4
A_HEAD_DIM = 64
A_PAIRS = A_HEADS // 2
BLOCK = 128
ATTN_TILE = 512
KEYS = 3 * BLOCK

B_HEADS = 8
B_HEAD_K = 128
B_HEAD_V = 128
CONV_WIDTH = 5
CONV_PAD = CONV_WIDTH // 2
CHUNK = 64
N_QKV = 3 * B_HEADS * B_HEAD_K
HALO = 8
PROJ_BLOCK = 256
PROJ_RING = 3

V7X_VMEM_LIMIT_BYTES = 56 * 1024 * 1024


def _sigmoid(x):
    return 1.0 / (1.0 + jnp.exp(-x))


def _silu(x):
    half = 0.5 * x
    return half + half * jnp.tanh(half)


def _softplus(x):
    return jnp.maximum(x, 0.0) + jnp.log(1.0 + jnp.exp(-jnp.abs(x)))


def _rms_rows(x, gain):
    ms = jnp.mean(x * x, axis=-1, keepdims=True)
    return x * lax.rsqrt(ms + EPS) * gain


def _split3(x):
    a = x.astype(BF16)
    r = x - a.astype(F32)
    b = r.astype(BF16)
    c = (r - b.astype(F32)).astype(BF16)
    return a, b, c


def _dot(a, b):
    return jnp.dot(a, b, preferred_element_type=F32)


def _dot_nt(a, b):
    return lax.dot_general(a, b, (((1,), (1,)), ((), ())), preferred_element_type=F32)


def _dot_tn(a, b):
    return lax.dot_general(a, b, (((0,), (0,)), ((), ())), preferred_element_type=F32)


def _params(semantics):
    return pltpu.CompilerParams(dimension_semantics=semantics, vmem_limit_bytes=V7X_VMEM_LIMIT_BYTES)


def _attn_proj_kernel(x_ref, gain_ref, wqt_ref, wk_ref, wvt_ref, wgt_ref, qt_ref, k_ref, vt_ref, gt_ref):
    hn = _rms_rows(x_ref[...], gain_ref[...]).astype(BF16)
    qt_ref[...] = (_dot_nt(wqt_ref[...], hn) * (A_HEAD_DIM ** -0.5 * LOG2E)).astype(BF16)
    k_ref[...] = _dot(hn, wk_ref[...]).astype(BF16)
    vt_ref[...] = _dot_nt(wvt_ref[...], hn).astype(BF16)
    gt_ref[...] = _dot_nt(wgt_ref[...], hn).astype(BF16)


def _attn_proj(x, gain, w_in):
    b, s, _ = x.shape
    tm = ATTN_TILE
    assert s % tm == 0
    kvw = A_KV_HEADS * A_HEAD_DIM
    wqt = w_in[:, :D_MODEL].T.astype(BF16)
    wk = w_in[:, D_MODEL:D_MODEL + kvw].reshape(D_MODEL, A_KV_HEADS, A_HEAD_DIM)
    wk = jnp.concatenate([wk, jnp.zeros_like(wk)], axis=2).reshape(D_MODEL, 2 * kvw).astype(BF16)
    wvt = w_in[:, D_MODEL + kvw:D_MODEL + 2 * kvw].T.astype(BF16)
    wgt = w_in[:, D_MODEL + 2 * kvw:].T.astype(BF16)
    full = lambda a: pl.BlockSpec(a.shape, lambda i, j: (0, 0), pipeline_mode=pl.Buffered(1))
    feat = lambda n: pl.BlockSpec((None, None, n, tm), lambda i, j: (i, j, 0, 0))
    feat_shape = lambda n: jax.ShapeDtypeStruct((b, s // tm, n, tm), BF16)
    return pl.pallas_call(
        _attn_proj_kernel,
        grid=(b, s // tm),
        in_specs=[pl.BlockSpec((None, tm, D_MODEL), lambda i, j: (i, j, 0)), full(gain),
                  full(wqt), full(wk), full(wvt), full(wgt)],
        out_specs=[feat(D_MODEL), pl.BlockSpec((None, tm, 2 * kvw), lambda i, j: (i, j, 0)), feat(kvw), feat(D_MODEL)],
        out_shape=[feat_shape(D_MODEL), jax.ShapeDtypeStruct((b, s, 2 * kvw), BF16), feat_shape(kvw),
                   feat_shape(D_MODEL)],
        compiler_params=_params(("parallel", "parallel")),
        name="attn_proj",
    )(x, gain, wqt, wk, wvt, wgt)


A_GROUP = A_HEADS // A_KV_HEADS
ONES_ROWS = 16


def _attn_core_kernel(sink_ref, qt_ref, kp_ref, kc_ref, kn_ref, vp_ref, vc_ref, vn_ref, gt_ref, x_ref, bias_ref,
                      w_ref, o_ref, k_scr, vt_scr, *, tq, n_tiles):
    j = pl.program_id(1)
    k_scr[0:BLOCK, :] = kp_ref[...]
    k_scr[BLOCK:BLOCK + tq, :] = kc_ref[...]
    k_scr[BLOCK + tq:, :] = kn_ref[...]
    vt_scr[:, 0:BLOCK] = vp_ref[...]
    vt_scr[:, BLOCK:BLOCK + tq] = vc_ref[...]
    vt_scr[:, BLOCK + tq:] = vn_ref[...]
    zero_rows = jnp.zeros((A_HEAD_DIM, BLOCK), BF16)
    ones_rows = jnp.ones((ONES_ROWS, KEYS), BF16)
    n_qb = tq // BLOCK

    def scores(qb, g):
        keys = slice(qb * BLOCK, qb * BLOCK + KEYS)
        qcols = slice(qb * BLOCK, (qb + 1) * BLOCK)
        q_t = jnp.concatenate(
            [jnp.concatenate([qt_ref[(A_GROUP * g + r) * A_HEAD_DIM:(A_GROUP * g + r + 1) * A_HEAD_DIM, qcols],
                              zero_rows], axis=0) for r in range(A_GROUP)], axis=1)
        gqb = j * n_qb + qb
        table = jnp.where(gqb == 0, 1, jnp.where(gqb == n_tiles * n_qb - 1, 2, 0)) * A_KV_HEADS
        return _dot(k_scr[keys, g * 128:(g + 1) * 128], q_t) + bias_ref[table + g]

    def out_chunk(qb, og_t, c):
        rows = slice(qb * BLOCK, (qb + 1) * BLOCK)
        cols = slice(c * 256, (c + 1) * 256)
        o_ref[rows, cols] = x_ref[rows, cols] + _dot_tn(og_t, w_ref[:, cols])

    items = [(qb, g) for qb in range(n_qb) for g in range(A_KV_HEADS)]
    pending = []
    og_parts = []
    s_next = scores(*items[0])
    for n, (qb, g) in enumerate(items):
        keys = slice(qb * BLOCK, qb * BLOCK + KEYS)
        qcols = slice(qb * BLOCK, (qb + 1) * BLOCK)
        s = s_next
        if n + 1 < len(items):
            s_next = scores(*items[n + 1])
        if pending:
            pending.pop(0)()
        sink = sink_ref[g]
        m = jnp.maximum(jnp.max(s, axis=0, keepdims=True), sink)
        probs = jnp.exp2(s - m).astype(BF16)
        v_t = jnp.concatenate([vt_scr[g * A_HEAD_DIM:(g + 1) * A_HEAD_DIM, keys], ones_rows], axis=0)
        o_ext = _dot(v_t, probs)
        inv = 1.0 / (o_ext[A_HEAD_DIM:A_HEAD_DIM + 1] + jnp.exp2(sink - m))
        for r in range(A_GROUP):
            h = A_GROUP * g + r
            lanes = slice(r * BLOCK, (r + 1) * BLOCK)
            gate = gt_ref[h * A_HEAD_DIM:(h + 1) * A_HEAD_DIM, qcols].astype(F32)
            og_parts.append((o_ext[:A_HEAD_DIM, lanes] * inv[:, lanes] * _silu(gate)).astype(BF16))
        if g == A_KV_HEADS - 1:
            og_t = jnp.concatenate(og_parts, axis=0)
            og_parts = []
            pending += [functools.partial(out_chunk, qb, og_t, c) for c in range(D_MODEL // 256)]
    for piece in pending:
        piece()


def _attn_tables():
    qpos = np.arange(BLOCK)[:, None]
    krel = np.arange(KEYS)[None, :] - BLOCK
    dist = np.abs(krel - qpos).astype(np.float32)
    slopes = np.power(2.0, -8.0 * np.arange(1, A_HEADS + 1) / A_HEADS).astype(np.float32)
    band = np.where(dist <= BLOCK, 0.0, -np.inf)
    per_head = (-slopes[:, None, None] * dist[None]).astype(np.float64) * LOG2E + band[None]
    sets = np.broadcast_to(per_head[None], (3, A_HEADS, BLOCK, KEYS)).copy()
    sets[1, :, :, :BLOCK] = -np.inf
    sets[2, :, :, 2 * BLOCK:] = -np.inf
    sets = sets.reshape(3, A_KV_HEADS, A_GROUP, BLOCK, KEYS).transpose(0, 1, 4, 2, 3)
    return jnp.asarray(sets.reshape(3 * A_KV_HEADS, KEYS, A_GROUP * BLOCK), F32)


def _attn_core(x, q_t, k, v_t, gate_t, sink, w_out):
    b, s, _ = x.shape
    tq = ATTN_TILE
    assert s % tq == 0 and s // BLOCK >= 2
    n_tiles = s // tq
    nb = s // BLOCK
    per = tq // BLOCK
    kw = k.shape[-1]
    vw = v_t.shape[2]
    bias = _attn_tables()
    sink_rows = jnp.repeat((sink * LOG2E).reshape(A_KV_HEADS, 1, A_GROUP), BLOCK, axis=2)
    tile = pl.BlockSpec((None, tq, D_MODEL), lambda i, j: (i, j, 0))
    feat = lambda n: pl.BlockSpec((None, None, n, tq), lambda i, j: (i, j, 0, 0))
    prev = lambda j: jnp.maximum(j * per - 1, 0)
    nxt = lambda j: jnp.minimum((j + 1) * per, nb - 1)
    const = lambda a: pl.BlockSpec(a.shape, lambda i, j: (0,) * a.ndim, pipeline_mode=pl.Buffered(1))
    kernel = functools.partial(_attn_core_kernel, tq=tq, n_tiles=n_tiles)
    return pl.pallas_call(
        kernel,
        grid=(b, n_tiles),
        in_specs=[
            const(sink_rows),
            feat(D_MODEL),
            pl.BlockSpec((None, BLOCK, kw), lambda i, j: (i, prev(j), 0)),
            pl.BlockSpec((None, tq, kw), lambda i, j: (i, j, 0)),
            pl.BlockSpec((None, BLOCK, kw), lambda i, j: (i, nxt(j), 0)),
            pl.BlockSpec((None, None, vw, BLOCK), lambda i, j: (i, jnp.maximum(j - 1, 0), 0, per - 1)),
            feat(vw),
            pl.BlockSpec((None, None, vw, BLOCK), lambda i, j: (i, jnp.minimum(j + 1, n_tiles - 1), 0, 0)),
            feat(D_MODEL),
            tile,
            const(bias),
            const(w_out),
        ],
        out_specs=tile,
        out_shape=jax.ShapeDtypeStruct(x.shape, F32),
        scratch_shapes=[pltpu.VMEM((tq + 2 * BLOCK, kw), BF16), pltpu.VMEM((vw, tq + 2 * BLOCK), BF16)],
        compiler_params=_params(("parallel", "parallel")),
        name="attn_core",
    )(sink_rows, q_t, k, k, k, v_t, v_t, v_t, gate_t, x, bias, w_out)


def _proj_stages(j, xp_ref, xc_ref, xn_ref, gain_ref, wqkv_ref, wg_ref, wba_ref, wbat_ref, conv_ref,
                 alog_ref, dtb_ref, alogc_ref, dtbc_ref, tril_ref, triu_ref, trilt_ref, triut_ref,
                 q_ref, k_ref, v_ref, gate_ref, col_ref, row_ref, hn_scr, proj_scr, *, tm, n_tiles):
    gain = gain_ref[...]
    keep_prev = (j > 0).astype(F32)
    keep_next = (j < n_tiles - 1).astype(F32)
    hn_scr[0:HALO, :] = (_rms_rows(xp_ref[...], gain) * keep_prev).astype(BF16)
    hn_scr[HALO:HALO + tm, :] = _rms_rows(xc_ref[...], gain).astype(BF16)
    hn_scr[HALO + tm:, :] = (_rms_rows(xn_ref[...], gain) * keep_next).astype(BF16)
    hn = hn_scr[HALO:HALO + tm, :]

    def gate_and_terms():
        gate_ref[...] = _dot(hn, wg_ref[...]).astype(BF16)
        ba = _dot(hn, wba_ref[...])
        g_c = -jnp.exp(alog_ref[...]) * _softplus(ba + dtb_ref[...])
        bat = _dot_nt(wbat_ref[...], hn)
        g_r = -jnp.exp(alogc_ref[...]) * _softplus(bat + dtbc_ref[...])
        return _sigmoid(ba), _sigmoid(bat), _split3(g_c), _split3(g_r)

    def decay_sums(beta_c, beta_r, gc3, gr3):
        ent_r = lax.broadcasted_iota(jnp.int32, (4 * B_HEADS, tm), 0)
        cum_r = jnp.where(ent_r < 3 * B_HEADS, sum(_dot(p, trilt_ref[...]) for p in gr3),
                          sum(_dot(p, triut_ref[...]) for p in gr3))
        row_ref[...] = jnp.where(ent_r < 2 * B_HEADS, beta_r, cum_r)
        ent_c = lax.broadcasted_iota(jnp.int32, (CHUNK, 4 * B_HEADS), 1)
        for c in range(tm // CHUNK):
            rows = slice(c * CHUNK, (c + 1) * CHUNK)
            fwd = sum(_dot(tril_ref[...], p[rows]) for p in gc3)
            bwd = sum(_dot(triu_ref[...], p[rows]) for p in gc3)
            col_ref[rows, :] = jnp.where(ent_c < 2 * B_HEADS, beta_c[rows], jnp.where(ent_c < 3 * B_HEADS, fwd, bwd))

    taps = conv_ref[...]
    hn_ext = hn_scr[...]
    n_blocks = N_QKV // PROJ_BLOCK
    rc = 128

    def project(n):
        proj_scr[n % PROJ_RING] = _dot(hn_ext, wqkv_ref[:, n * PROJ_BLOCK:(n + 1) * PROJ_BLOCK])

    def conv(n):
        for half in range(PROJ_BLOCK // 128):
            cb = n * (PROJ_BLOCK // 128) + half
            cols = slice(cb * 128, (cb + 1) * 128)
            out_ref, head = ((q_ref, k_ref, v_ref)[cb // B_HEADS], cb % B_HEADS)
            for r0 in range(0, tm, rc):
                window = proj_scr[n % PROJ_RING, r0:r0 + rc + 2 * HALO, half * 128:(half + 1) * 128]
                acc = None
                for t in range(CONV_WIDTH):
                    shift = CONV_PAD - t
                    moved = window if shift == 0 else pltpu.roll(window, shift % (rc + 2 * HALO), axis=0)
                    term = taps[t:t + 1, cols] * moved[HALO:HALO + rc]
                    acc = term if acc is None else acc + term
                y = _silu(acc)
                if out_ref is not v_ref:
                    scale = 1.0 if out_ref is k_ref else B_HEAD_K ** -0.5
                    y = y * (lax.rsqrt(jnp.sum(y * y, axis=-1, keepdims=True) + EPS) * scale)
                out_ref[r0:r0 + rc, head * 128:(head + 1) * 128] = y.astype(BF16)

    project(0)
    terms = gate_and_terms()
    yield
    for n in range(n_blocks):
        if n + 1 < n_blocks:
            project(n + 1)
        yield
        conv(n)
        if n == 1:
            decay_sums(*terms)


def _delta_front_kernel(*refs, tm, n_tiles, n_total):
    n_in, n_out = 17, 7
    ins, outs, scr = refs[:n_in], refs[n_in:n_in + n_out], refs[n_in + n_out:]
    gate_ref, w_ref, u_ref, qg_ref, kd_ref, qk_ref, egl_ref = outs
    hn_scr, proj_scr, q_scr, k_scr, v_scr, col_scr, row_scr = scr
    j = pl.program_id(0)

    @pl.when(j == 0)
    def _():
        for ref in (q_scr, k_scr, v_scr, col_scr, row_scr):
            ref[1] = jnp.zeros(ref.shape[1:], ref.dtype)

    cur = j % 2
    prev = 1 - cur
    tile = lax.rem(jnp.minimum(j, n_total - 1), n_tiles)
    proj = _proj_stages(tile, *ins, q_scr.at[cur], k_scr.at[cur], v_scr.at[cur], gate_ref, col_scr.at[cur],
                        row_scr.at[cur], hn_scr, proj_scr, tm=tm, n_tiles=n_tiles)
    n_chunks = tm // CHUNK
    waves = [range(c0, c0 + PREP_WAVE) for c0 in range(0, n_chunks, PREP_WAVE)]
    prep = itertools.chain.from_iterable(
        _prep_stages(q_scr.at[prev], k_scr.at[prev], v_scr.at[prev], col_scr.at[prev], row_scr.at[prev],
                     w_ref, u_ref, qg_ref, kd_ref, qk_ref, egl_ref, chunk_ids=wave) for wave in waves)
    live = {"proj": proj, "prep": prep}
    turn = 0
    while live:
        for name in ("proj", "prep", "prep") if turn % 2 else ("proj", "prep"):
            if name in live and next(live[name], StopIteration) is StopIteration:
                del live[name]
        turn += 1


def _delta_front(x1, gain, w_in, conv_w, a_log, dt_bias):
    b, s, _ = x1.shape
    tm = FRONT_TILE
    assert s % tm == 0
    n_tiles = s // tm
    nc = s // CHUNK
    per = tm // HALO
    nh = s // HALO
    kw = B_HEADS * B_HEAD_K
    wqkv = w_in[:, :N_QKV].astype(BF16)
    wg = w_in[:, N_QKV:N_QKV + kw].astype(BF16)
    wba = w_in[:, N_QKV + kw:].astype(BF16)
    wbat = wba.T
    pad = jnp.zeros((1, 2 * B_HEADS), F32)
    alog = jnp.concatenate([pad, a_log.reshape(1, 2 * B_HEADS).astype(F32)], axis=1)
    dtb = jnp.concatenate([pad, dt_bias.reshape(1, 2 * B_HEADS).astype(F32)], axis=1)
    i = np.arange(CHUNK)
    tril = (i[:, None] >= i[None, :]).astype(np.float32)
    triu = tril.T
    eye = np.eye(tm // CHUNK, dtype=np.float32)
    trilt = jnp.asarray(np.kron(eye, tril.T), BF16)
    triut = jnp.asarray(np.kron(eye, triu.T), BF16)
    tril = jnp.asarray(tril, BF16)
    triu = jnp.asarray(triu, BF16)
    full = lambda a: pl.BlockSpec(a.shape, lambda j_: (0,) * a.ndim, pipeline_mode=pl.Buffered(1))
    n_total = b * n_tiles
    cur = lambda j_: divmod(jnp.minimum(j_, n_total - 1), n_tiles)
    done = lambda j_: divmod(jnp.maximum(j_ - 1, 0), n_tiles)
    both = lambda n: pl.BlockSpec((None, 2, tm, n), lambda j_: (done(j_)[0], 0, done(j_)[1], 0))
    act = jax.ShapeDtypeStruct((b, 2, s, kw), BF16)
    kernel = functools.partial(_delta_front_kernel, tm=tm, n_tiles=n_tiles, n_total=n_total)
    consts = [gain, wqkv, wg, wba, wbat, conv_w.astype(F32), alog, dtb, alog.T, dtb.T, tril, triu, trilt, triut]
    return pl.pallas_call(
        kernel,
        grid=(n_total + 1,),
        in_specs=[
            pl.BlockSpec((None, HALO, D_MODEL), lambda j_: (cur(j_)[0], jnp.maximum(cur(j_)[1] * per - 1, 0), 0)),
            pl.BlockSpec((None, tm, D_MODEL), lambda j_: (cur(j_)[0], cur(j_)[1], 0)),
            pl.BlockSpec((None, HALO, D_MODEL),
                         lambda j_: (cur(j_)[0], jnp.minimum((cur(j_)[1] + 1) * per, nh - 1), 0)),
        ] + [full(a) for a in consts],
        out_specs=[pl.BlockSpec((None, tm, kw), lambda j_: (cur(j_)[0], cur(j_)[1], 0)),
                   both(kw), both(kw), both(kw), both(kw), both(B_HEADS * CHUNK),
                   pl.BlockSpec((None, 2, tm // CHUNK, 1, kw), lambda j_: (done(j_)[0], 0, done(j_)[1], 0, 0))],
        out_shape=[jax.ShapeDtypeStruct((b, s, kw), BF16), act, act, act, act,
                   jax.ShapeDtypeStruct((b, 2, s, B_HEADS * CHUNK), BF16),
                   jax.ShapeDtypeStruct((b, 2, nc, 1, kw), F32)],
        scratch_shapes=[pltpu.VMEM((tm + 2 * HALO, D_MODEL), BF16),
                        pltpu.VMEM((PROJ_RING, tm + 2 * HALO, PROJ_BLOCK), F32),
                        pltpu.VMEM((2, tm, kw), BF16), pltpu.VMEM((2, tm, kw), BF16), pltpu.VMEM((2, tm, kw), BF16),
                        pltpu.VMEM((2, tm, 4 * B_HEADS), F32), pltpu.VMEM((2, 4 * B_HEADS, tm), F32)],
        compiler_params=_params(("arbitrary",)),
        name="delta_front",
    )(x1, x1, x1, *consts)


FRONT_TILE = 256
PREP_WAVE = 4
GROUP = 4
N_GROUPS = B_HEADS // GROUP
PAIR = 2
N_PAIRS = B_HEADS // PAIR


def _block_diag(x, n_blocks):
    r, width = x.shape
    shift = (width // n_blocks).bit_length() - 1
    blk = lax.broadcasted_iota(jnp.int32, (r, width), 1) >> shift
    zero = jnp.zeros_like(x)
    return jnp.concatenate([jnp.where(blk == a, x, zero) for a in range(n_blocks)], axis=0)


def _prep_stages(q_ref, k_ref, v_ref, col_ref, row_ref, w_ref, u_ref, qg_ref, kd_ref, qk_ref, egl_ref, *, chunk_ids):
    width = GROUP * CHUNK

    def row_piece(kind, c, g):
        first = kind * B_HEADS + g * GROUP
        return jnp.concatenate([row_ref[first + a:first + a + 1, c * CHUNK:(c + 1) * CHUNK] for a in range(GROUP)],
                               axis=1)

    ri = lax.broadcasted_iota(jnp.int32, (CHUNK, width), 0)
    cj = lax.broadcasted_iota(jnp.int32, (CHUNK, width), 1) & (CHUNK - 1)
    low_half = lax.broadcasted_iota(jnp.int32, (CHUNK, 128), 1) < CHUNK
    eye = (ri == cj).astype(F32)
    masks = ((ri >= cj, ri > cj, CHUNK - 1), (ri <= cj, ri < cj, 0))

    def side_by_side(parts):
        return jnp.concatenate([jnp.where(low_half, parts[0], parts[1]), jnp.where(low_half, parts[2], parts[3])],
                               axis=1)

    gram, kbd = {}, {}
    for c in chunk_ids:
        rows = slice(c * CHUNK, (c + 1) * CHUNK)
        for g in range(N_GROUPS):
            cols = slice(g * 512, (g + 1) * 512)
            k4 = k_ref[rows, cols]
            kbd[c, g] = _block_diag(k4, GROUP)
            gram[c, g] = _dot_nt(jnp.concatenate([q_ref[rows, cols], k4], axis=0), kbd[c, g])
    yield

    x, p, dq, dk = {}, {}, {}, {}
    is_eye = ri == cj
    for c in chunk_ids:
        rows = slice(c * CHUNK, (c + 1) * CHUNK)
        for d, (incl, strict, last) in enumerate(masks):
            for g in range(N_GROUPS):
                beta_cols, gc_cols, last_rows = [], [], []
                for a in range(GROUP):
                    h = g * GROUP + a
                    idx = d * B_HEADS + h
                    hcols = slice(h * 128, (h + 1) * 128)
                    beta_cols.append(jnp.broadcast_to(col_ref[rows, idx:idx + 1], (CHUNK, 128)))
                    gc_cols.append(jnp.broadcast_to(col_ref[rows, 2 * B_HEADS + idx:2 * B_HEADS + idx + 1],
                                                    (CHUNK, 128)))
                    t_last = c * CHUNK + last
                    g_last = row_ref[2 * B_HEADS + idx:2 * B_HEADS + idx + 1, t_last:t_last + 1]
                    egl_ref[d, c, :, hcols] = jnp.exp(jnp.broadcast_to(g_last, (1, 128)))
                    last_rows.append(jnp.broadcast_to(g_last, (1, CHUNK)))
                lanes = slice(g * width, (g + 1) * width)
                gc_row = row_piece(2 + d, c, g)
                gc_col = side_by_side(gc_cols)
                dq[c, d, g] = jnp.where(is_eye, jnp.exp(gc_col), 0.0).astype(BF16)
                dk[c, d, g] = jnp.where(is_eye, jnp.exp(jnp.concatenate(last_rows, axis=1) - gc_col), 0.0).astype(BF16)
                decay = jnp.where(incl, jnp.exp(jnp.where(incl, gc_col - gc_row, 0.0)), 0.0)
                gm = gram[c, g]
                qk_ref[d, rows, lanes] = (gm[:CHUNK] * decay).astype(BF16)
                low = jnp.where(strict, gm[CHUNK:] * decay * side_by_side(beta_cols), 0.0)
                x[c, d, g] = -low
                p[c, d, g] = eye - low
        if c % 2 == 1:
            yield

    keys = list(x)
    for key in keys:
        xb = x[key].astype(BF16)
        x[key] = _dot(xb, _block_diag(xb, GROUP))
    yield
    squarings = CHUNK.bit_length() - 2
    for _ in range(squarings - 1):
        for key in keys:
            xb = x[key].astype(BF16)
            z = _dot(jnp.concatenate([xb, p[key].astype(BF16)], axis=0), _block_diag(xb, GROUP))
            x[key] = z[:CHUNK]
            p[key] = p[key] + z[CHUNK:]
        yield
    for key in keys:
        p[key] = p[key] + _dot(p[key].astype(BF16), _block_diag(x[key].astype(BF16), GROUP))
    yield

    for c in chunk_ids:
        rows = slice(c * CHUNK, (c + 1) * CHUNK)
        for g in range(N_GROUPS):
            cols = slice(g * 512, (g + 1) * 512)
            tb = [p[c, d, g] * row_piece(d, c, g) for d in range(2)]
            tbg = [tb[d] * jnp.exp(row_piece(2 + d, c, g)) for d in range(2)]
            uu = _dot(jnp.concatenate(tb, axis=0).astype(BF16), _block_diag(v_ref[rows, cols], GROUP))
            k_lhs = jnp.concatenate([jnp.concatenate(tbg, axis=0).astype(BF16), dk[c, 0, g], dk[c, 1, g]], axis=0)
            ww = _dot(k_lhs, kbd[c, g])
            qq = _dot(jnp.concatenate([dq[c, 0, g], dq[c, 1, g]], axis=0), _block_diag(q_ref[rows, cols], GROUP))
            for d in range(2):
                u_ref[d, rows, cols] = uu[d * CHUNK:(d + 1) * CHUNK].astype(BF16)
                w_ref[d, rows, cols] = ww[d * CHUNK:(d + 1) * CHUNK].astype(BF16)
                kd_ref[d, rows, cols] = ww[(2 + d) * CHUNK:(3 + d) * CHUNK].astype(BF16)
                qg_ref[d, rows, cols] = qq[d * CHUNK:(d + 1) * CHUNK].astype(BF16)
    yield


SCAN_CHUNKS = 2


def _delta_scan_kernel(*refs):
    per_dir = 6
    dirs = (refs[:per_dir] + (refs[2 * per_dir],), refs[per_dir:2 * per_dir] + (refs[2 * per_dir + 1],))
    state_ref = refs[2 * per_dir + 2]
    n_batch = dirs[0][0].shape[0]

    @pl.when(pl.program_id(0) == 0)
    def _():
        state_ref[...] = jnp.zeros_like(state_ref)

    tile = PAIR * B_HEAD_K
    units = [(i, d, hp) for i in range(n_batch) for d in range(2) for hp in range(N_PAIRS)]
    slot = lambda i, d, hp: (i * 2 + d) * N_PAIRS + hp
    for step in range(SCAN_CHUNKS):
        sub = (step, SCAN_CHUNKS - 1 - step)
        rows = [slice(c * CHUNK, (c + 1) * CHUNK) for c in sub]
        rs = {}
        for i, d, hp in units:
            w_ref, _, qg_ref = dirs[d][:3]
            cols = slice(hp * tile, (hp + 1) * tile)
            lhs = jnp.concatenate([w_ref[i, rows[d], cols], qg_ref[i, rows[d], cols]], axis=0)
            rs[i, d, hp] = _dot(lhs, state_ref[slot(i, d, hp)].astype(BF16))
        for i, d, hp in units:
            _, u_ref, _, kd_ref, qk_ref, egl_ref, o_ref = dirs[d]
            cols = slice(hp * tile, (hp + 1) * tile)
            r = rs[i, d, hp]
            v_new = (u_ref[i, rows[d], cols].astype(F32) - r[:CHUNK]).astype(BF16)
            qk = qk_ref[i, rows[d], hp * PAIR * CHUNK:(hp + 1) * PAIR * CHUNK]
            o_ref[i, rows[d], cols] = (r[CHUNK:] + _dot(qk, _block_diag(v_new, PAIR))).astype(o_ref.dtype)
            idx = slot(i, d, hp)
            for a in range(PAIR):
                blk = slice(a * B_HEAD_K, (a + 1) * B_HEAD_K)
                hcols = slice(hp * tile + a * B_HEAD_K, hp * tile + (a + 1) * B_HEAD_K)
                update = _dot_tn(kd_ref[i, rows[d], hcols], v_new[:, blk])
                state_ref[idx, blk, blk] = state_ref[idx, blk, blk] * egl_ref[i, sub[d], :, hcols] + update


def _delta_scan(w, u, qg, kd, qk, egl):
    b, _, s, kw = w.shape
    rows = SCAN_CHUNKS * CHUNK
    assert s % rows == 0
    n_steps = s // rows
    chunk_of = (lambda c: c, lambda c: n_steps - 1 - c)
    in_specs = []
    for d in range(2):
        act = lambda n, d=d: pl.BlockSpec((b, None, rows, n), lambda c: (0, d, chunk_of[d](c), 0))
        in_specs += [act(kw), act(kw), act(kw), act(kw), act(B_HEADS * CHUNK),
                     pl.BlockSpec((b, None, SCAN_CHUNKS, 1, kw), lambda c, d=d: (0, d, chunk_of[d](c), 0, 0))]
    out = jax.ShapeDtypeStruct((b, s, kw), BF16)
    out_specs = [pl.BlockSpec((b, rows, kw), lambda c, d=d: (0, chunk_of[d](c), 0)) for d in range(2)]
    args = (w, u, qg, kd, qk, egl)
    return pl.pallas_call(
        _delta_scan_kernel,
        grid=(n_steps,),
        in_specs=in_specs,
        out_specs=out_specs,
        out_shape=[out, out],
        scratch_shapes=[pltpu.VMEM((b * 2 * N_PAIRS, PAIR * B_HEAD_K, PAIR * B_HEAD_V), F32)],
        compiler_params=_params(("arbitrary",)),
        name="delta_scan",
    )(*args, *args)


OUT_TILE = 1024


def _delta_out_kernel(of_ref, ob_ref, gate_ref, x_ref, onorm_ref, w_ref, fnorm_ref, o_ref, og_scr):
    onorm = onorm_ref[...]
    for h in range(B_HEADS):
        cols = slice(h * 128, (h + 1) * 128)
        o = of_ref[:, cols].astype(F32) + ob_ref[:, cols].astype(F32)
        o = o * lax.rsqrt(jnp.mean(o * o, axis=-1, keepdims=True) + EPS) * onorm
        gate = gate_ref[:, cols].astype(F32)
        og_scr[:, cols] = (o * _silu(gate)).astype(BF16)
    y = x_ref[...] + _dot(og_scr[...], w_ref[...])
    o_ref[...] = _rms_rows(y, fnorm_ref[...])


def _delta_out(o_f, o_b, gate, x1, out_norm, w_out, final_norm):
    t = x1.shape[0]
    tm = OUT_TILE
    assert t % tm == 0
    row = pl.BlockSpec((tm, D_MODEL), lambda i: (i, 0))
    vec = lambda n: pl.BlockSpec((1, n), lambda i: (0, 0))
    return pl.pallas_call(
        _delta_out_kernel,
        grid=(t // tm,),
        in_specs=[row, row, row, row, vec(B_HEAD_V), pl.BlockSpec(w_out.shape, lambda i: (0, 0)), vec(D_MODEL)],
        out_specs=row,
        out_shape=jax.ShapeDtypeStruct((t, D_MODEL), F32),
        scratch_shapes=[pltpu.VMEM((tm, D_MODEL), BF16)],
        compiler_params=_params(("parallel",)),
        name="delta_out",
    )(o_f, o_b, gate, x1, out_norm, w_out, final_norm)


def kernel(x, attn_norm, attn_w_in, attn_sink, attn_w_out, delta_norm, delta_w_in, delta_conv, delta_a_log,
           delta_dt_bias, delta_out_norm, delta_w_out, final_norm):
    b, s, d = x.shape
    assert d == D_MODEL
    t = b * s
    q_t, k, v_t, gate_t = _attn_proj(x, attn_norm.reshape(1, d).astype(F32), attn_w_in)
    x1 = _attn_core(x, q_t, k, v_t, gate_t, attn_sink.astype(F32), attn_w_out.astype(BF16))
    dgate, *operators = _delta_front(x1, delta_norm.reshape(1, d).astype(F32), delta_w_in, delta_conv,
                                     delta_a_log, delta_dt_bias)
    o_f, o_b = _delta_scan(*operators)
    out = _delta_out(o_f.reshape(t, d), o_b.reshape(t, d), dgate.reshape(t, d), x1.reshape(t, d),
                     delta_out_norm.reshape(1, B_HEAD_V).astype(F32), delta_w_out.astype(BF16),
                     final_norm.reshape(1, d).astype(F32))
    return out.reshape(b, s, d)
```

```python
import functools
import itertools

import numpy as np
import jax
import jax.numpy as jnp
from jax import lax
from jax.experimental import pallas as pl
from jax.experimental.pallas import tpu as pltpu

F32 = jnp.float32
BF16 = jnp.bfloat16
EPS = 1e-6
LOG2E = float(np.log2(np.e))

D_MODEL = 1024
A_HEADS = 16
A_KV_HEADS = 4
A_HEAD_DIM = 64
A_PAIRS = A_HEADS // 2
BLOCK = 128
ATTN_TILE = 512
KEYS = 3 * BLOCK

B_HEADS = 8
B_HEAD_K = 128
B_HEAD_V = 128
CONV_WIDTH = 5
CONV_PAD = CONV_WIDTH // 2
CHUNK = 64
N_QKV = 3 * B_HEADS * B_HEAD_K
HALO = 8
PROJ_BLOCK = 256
PROJ_RING = 3

V7X_VMEM_LIMIT_BYTES = 56 * 1024 * 1024


def _sigmoid(x):
    return 1.0 / (1.0 + jnp.exp(-x))


def _silu(x):
    half = 0.5 * x
    return half + half * jnp.tanh(half)


def _softplus(x):
    return jnp.maximum(x, 0.0) + jnp.log(1.0 + jnp.exp(-jnp.abs(x)))


def _rms_rows(x, gain):
    ms = jnp.mean(x * x, axis=-1, keepdims=True)
    return x * lax.rsqrt(ms + EPS) * gain


def _split3(x):
    a = x.astype(BF16)
    r = x - a.astype(F32)
    b = r.astype(BF16)
    c = (r - b.astype(F32)).astype(BF16)
    return a, b, c


def _dot(a, b):
    return jnp.dot(a, b, preferred_element_type=F32)


def _dot_nt(a, b):
    return lax.dot_general(a, b, (((1,), (1,)), ((), ())), preferred_element_type=F32)


def _dot_tn(a, b):
    return lax.dot_general(a, b, (((0,), (0,)), ((), ())), preferred_element_type=F32)


def _params(semantics, fuse_inputs=None):
    return pltpu.CompilerParams(dimension_semantics=semantics, vmem_limit_bytes=V7X_VMEM_LIMIT_BYTES,
                                allow_input_fusion=fuse_inputs)


def _attn_proj_kernel(x_ref, gain_ref, wqt_ref, wk_ref, wvt_ref, wgt_ref, qt_ref, k_ref, vt_ref, gt_ref):
    hn = _rms_rows(x_ref[...], gain_ref[...]).astype(BF16)
    qt_ref[...] = (_dot_nt(wqt_ref[...], hn) * (A_HEAD_DIM ** -0.5 * LOG2E)).astype(BF16)
    k_ref[...] = _dot(hn, wk_ref[...]).astype(BF16)
    vt_ref[...] = _dot_nt(wvt_ref[...], hn).astype(BF16)
    gt_ref[...] = _dot_nt(wgt_ref[...], hn).astype(BF16)


def _attn_proj(x, gain, w_in):
    b, s, _ = x.shape
    tm = ATTN_TILE
    assert s % tm == 0
    kvw = A_KV_HEADS * A_HEAD_DIM
    wqt = w_in[:, :D_MODEL].T.astype(BF16)
    wk = w_in[:, D_MODEL:D_MODEL + kvw].reshape(D_MODEL, A_KV_HEADS, A_HEAD_DIM)
    wk = jnp.concatenate([wk, jnp.zeros_like(wk)], axis=2).reshape(D_MODEL, 2 * kvw).astype(BF16)
    wvt = w_in[:, D_MODEL + kvw:D_MODEL + 2 * kvw].T.astype(BF16)
    wgt = w_in[:, D_MODEL + 2 * kvw:].T.astype(BF16)
    full = lambda a: pl.BlockSpec(a.shape, lambda i, j: (0, 0), pipeline_mode=pl.Buffered(1))
    feat = lambda n: pl.BlockSpec((None, None, n, tm), lambda i, j: (i, j, 0, 0))
    feat_shape = lambda n: jax.ShapeDtypeStruct((b, s // tm, n, tm), BF16)
    return pl.pallas_call(
        _attn_proj_kernel,
        grid=(b, s // tm),
        in_specs=[pl.BlockSpec((None, tm, D_MODEL), lambda i, j: (i, j, 0)), full(gain),
                  full(wqt), full(wk), full(wvt), full(wgt)],
        out_specs=[feat(D_MODEL), pl.BlockSpec((None, tm, 2 * kvw), lambda i, j: (i, j, 0)), feat(kvw), feat(D_MODEL)],
        out_shape=[feat_shape(D_MODEL), jax.ShapeDtypeStruct((b, s, 2 * kvw), BF16), feat_shape(kvw),
                   feat_shape(D_MODEL)],
        compiler_params=_params(("parallel", "parallel"), fuse_inputs=[False, False, True, True, True, True]),
        name="attn_proj",
    )(x, gain, wqt, wk, wvt, wgt)


A_GROUP = A_HEADS // A_KV_HEADS
ONES_ROWS = 16


def _attn_core_kernel(sink_ref, qt_ref, kp_ref, kc_ref, kn_ref, vp_ref, vc_ref, vn_ref, gt_ref, x_ref, bias_ref,
                      w_ref, o_ref, k_scr, vt_scr, *, tq, n_tiles):
    j = pl.program_id(1)
    k_scr[0:BLOCK, :] = kp_ref[...]
    k_scr[BLOCK:BLOCK + tq, :] = kc_ref[...]
    k_scr[BLOCK + tq:, :] = kn_ref[...]
    vt_scr[:, 0:BLOCK] = vp_ref[...]
    vt_scr[:, BLOCK:BLOCK + tq] = vc_ref[...]
    vt_scr[:, BLOCK + tq:] = vn_ref[...]
    zero_rows = jnp.zeros((A_HEAD_DIM, BLOCK), BF16)
    ones_rows = jnp.ones((ONES_ROWS, KEYS), BF16)
    n_qb = tq // BLOCK

    def scores(qb, g):
        keys = slice(qb * BLOCK, qb * BLOCK + KEYS)
        qcols = slice(qb * BLOCK, (qb + 1) * BLOCK)
        q_t = jnp.concatenate(
            [jnp.concatenate([qt_ref[(A_GROUP * g + r) * A_HEAD_DIM:(A_GROUP * g + r + 1) * A_HEAD_DIM, qcols],
                              zero_rows], axis=0) for r in range(A_GROUP)], axis=1)
        gqb = j * n_qb + qb
        table = jnp.where(gqb == 0, 1, jnp.where(gqb == n_tiles * n_qb - 1, 2, 0)) * A_KV_HEADS
        return _dot(k_scr[keys, g * 128:(g + 1) * 128], q_t) + bias_ref[table + g]

    def out_chunk(qb, og_t, c):
        rows = slice(qb * BLOCK, (qb + 1) * BLOCK)
        cols = slice(c * 256, (c + 1) * 256)
        o_ref[rows, cols] = x_ref[rows, cols] + _dot_tn(og_t, w_ref[:, cols])

    items = [(qb, g) for qb in range(n_qb) for g in range(A_KV_HEADS)]
    pending = []
    og_parts = []
    s_next = scores(*items[0])
    for n, (qb, g) in enumerate(items):
        keys = slice(qb * BLOCK, qb * BLOCK + KEYS)
        qcols = slice(qb * BLOCK, (qb + 1) * BLOCK)
        s = s_next
        if n + 1 < len(items):
            s_next = scores(*items[n + 1])
        if pending:
            pending.pop(0)()
        sink = sink_ref[g]
        m = jnp.maximum(jnp.max(s, axis=0, keepdims=True), sink)
        probs = jnp.exp2(s - m).astype(BF16)
        v_t = jnp.concatenate([vt_scr[g * A_HEAD_DIM:(g + 1) * A_HEAD_DIM, keys], ones_rows], axis=0)
        o_ext = _dot(v_t, probs)
        inv = 1.0 / (o_ext[A_HEAD_DIM:A_HEAD_DIM + 1] + jnp.exp2(sink - m))
        for r in range(A_GROUP):
            h = A_GROUP * g + r
            lanes = slice(r * BLOCK, (r + 1) * BLOCK)
            gate = gt_ref[h * A_HEAD_DIM:(h + 1) * A_HEAD_DIM, qcols].astype(F32)
            og_parts.append((o_ext[:A_HEAD_DIM, lanes] * inv[:, lanes] * _silu(gate)).astype(BF16))
        if g == A_KV_HEADS - 1:
            og_t = jnp.concatenate(og_parts, axis=0)
            og_parts = []
            pending += [functools.partial(out_chunk, qb, og_t, c) for c in range(D_MODEL // 256)]
    for piece in pending:
        piece()


def _attn_tables():
    qpos = np.arange(BLOCK)[:, None]
    krel = np.arange(KEYS)[None, :] - BLOCK
    dist = np.abs(krel - qpos).astype(np.float32)
    slopes = np.power(2.0, -8.0 * np.arange(1, A_HEADS + 1) / A_HEADS).astype(np.float32)
    band = np.where(dist <= BLOCK, 0.0, -np.inf)
    per_head = (-slopes[:, None, None] * dist[None]).astype(np.float64) * LOG2E + band[None]
    sets = np.broadcast_to(per_head[None], (3, A_HEADS, BLOCK, KEYS)).copy()
    sets[1, :, :, :BLOCK] = -np.inf
    sets[2, :, :, 2 * BLOCK:] = -np.inf
    sets = sets.reshape(3, A_KV_HEADS, A_GROUP, BLOCK, KEYS).transpose(0, 1, 4, 2, 3)
    return jnp.asarray(sets.reshape(3 * A_KV_HEADS, KEYS, A_GROUP * BLOCK), F32)


def _attn_core(x, q_t, k, v_t, gate_t, sink, w_out):
    b, s, _ = x.shape
    tq = ATTN_TILE
    assert s % tq == 0 and s // BLOCK >= 2
    n_tiles = s // tq
    nb = s // BLOCK
    per = tq // BLOCK
    kw = k.shape[-1]
    vw = v_t.shape[2]
    bias = _attn_tables()
    sink_rows = jnp.repeat((sink * LOG2E).reshape(A_KV_HEADS, 1, A_GROUP), BLOCK, axis=2)
    tile = pl.BlockSpec((None, tq, D_MODEL), lambda i, j: (i, j, 0))
    feat = lambda n: pl.BlockSpec((None, None, n, tq), lambda i, j: (i, j, 0, 0))
    prev = lambda j: jnp.maximum(j * per - 1, 0)
    nxt = lambda j: jnp.minimum((j + 1) * per, nb - 1)
    const = lambda a: pl.BlockSpec(a.shape, lambda i, j: (0,) * a.ndim, pipeline_mode=pl.Buffered(1))
    kernel = functools.partial(_attn_core_kernel, tq=tq, n_tiles=n_tiles)
    return pl.pallas_call(
        kernel,
        grid=(b, n_tiles),
        in_specs=[
            const(sink_rows),
            feat(D_MODEL),
            pl.BlockSpec((None, BLOCK, kw), lambda i, j: (i, prev(j), 0)),
            pl.BlockSpec((None, tq, kw), lambda i, j: (i, j, 0)),
            pl.BlockSpec((None, BLOCK, kw), lambda i, j: (i, nxt(j), 0)),
            pl.BlockSpec((None, None, vw, BLOCK), lambda i, j: (i, jnp.maximum(j - 1, 0), 0, per - 1)),
            feat(vw),
            pl.BlockSpec((None, None, vw, BLOCK), lambda i, j: (i, jnp.minimum(j + 1, n_tiles - 1), 0, 0)),
            feat(D_MODEL),
            tile,
            const(bias),
            const(w_out),
        ],
        out_specs=tile,
        out_shape=jax.ShapeDtypeStruct(x.shape, F32),
        scratch_shapes=[pltpu.VMEM((tq + 2 * BLOCK, kw), BF16), pltpu.VMEM((vw, tq + 2 * BLOCK), BF16)],
        compiler_params=_params(("parallel", "parallel")),
        name="attn_core",
    )(sink_rows, q_t, k, k, k, v_t, v_t, v_t, gate_t, x, bias, w_out)


def _proj_stages(j, xp_ref, xc_ref, xn_ref, gain_ref, wqkv_ref, wg_ref, wba_ref, wbat_ref, conv_ref,
                 alog_ref, dtb_ref, alogc_ref, dtbc_ref, tril_ref, triu_ref, trilt_ref, triut_ref,
                 q_ref, k_ref, v_ref, gate_ref, col_ref, row_ref, hn_scr, proj_scr, *, tm, n_tiles):
    gain = gain_ref[...]
    keep_prev = (j > 0).astype(F32)
    keep_next = (j < n_tiles - 1).astype(F32)
    hn_scr[0:HALO, :] = (_rms_rows(xp_ref[...], gain) * keep_prev).astype(BF16)
    hn_scr[HALO:HALO + tm, :] = _rms_rows(xc_ref[...], gain).astype(BF16)
    hn_scr[HALO + tm:, :] = (_rms_rows(xn_ref[...], gain) * keep_next).astype(BF16)
    hn = hn_scr[HALO:HALO + tm, :]

    def gate_and_terms():
        gate_ref[...] = _dot(hn, wg_ref[...]).astype(BF16)
        ba = _dot(hn, wba_ref[...])
        g_c = -jnp.exp(alog_ref[...]) * _softplus(ba + dtb_ref[...])
        bat = _dot_nt(wbat_ref[...], hn)
        g_r = -jnp.exp(alogc_ref[...]) * _softplus(bat + dtbc_ref[...])
        return _sigmoid(ba), _sigmoid(bat), _split3(g_c), _split3(g_r)

    def decay_sums(beta_c, beta_r, gc3, gr3):
        ent_r = lax.broadcasted_iota(jnp.int32, (4 * B_HEADS, tm), 0)
        cum_r = jnp.where(ent_r < 3 * B_HEADS, sum(_dot(p, trilt_ref[...]) for p in gr3),
                          sum(_dot(p, triut_ref[...]) for p in gr3))
        row_ref[...] = jnp.where(ent_r < 2 * B_HEADS, beta_r, cum_r)
        ent_c = lax.broadcasted_iota(jnp.int32, (CHUNK, 4 * B_HEADS), 1)
        for c in range(tm // CHUNK):
            rows = slice(c * CHUNK, (c + 1) * CHUNK)
            fwd = sum(_dot(tril_ref[...], p[rows]) for p in gc3)
            bwd = sum(_dot(triu_ref[...], p[rows]) for p in gc3)
            col_ref[rows, :] = jnp.where(ent_c < 2 * B_HEADS, beta_c[rows], jnp.where(ent_c < 3 * B_HEADS, fwd, bwd))

    taps = conv_ref[...]
    hn_ext = hn_scr[...]
    n_blocks = N_QKV // PROJ_BLOCK
    rc = 128

    def project(n):
        proj_scr[n % PROJ_RING] = _dot(hn_ext, wqkv_ref[:, n * PROJ_BLOCK:(n + 1) * PROJ_BLOCK])

    def conv(n):
        for half in range(PROJ_BLOCK // 128):
            cb = n * (PROJ_BLOCK // 128) + half
            cols = slice(cb * 128, (cb + 1) * 128)
            out_ref, head = ((q_ref, k_ref, v_ref)[cb // B_HEADS], cb % B_HEADS)
            for r0 in range(0, tm, rc):
                window = proj_scr[n % PROJ_RING, r0:r0 + rc + 2 * HALO, half * 128:(half + 1) * 128]
                acc = None
                for t in range(CONV_WIDTH):
                    shift = CONV_PAD - t
                    moved = window if shift == 0 else pltpu.roll(window, shift % (rc + 2 * HALO), axis=0)
                    term = taps[t:t + 1, cols] * moved[HALO:HALO + rc]
                    acc = term if acc is None else acc + term
                y = _silu(acc)
                if out_ref is not v_ref:
                    scale = 1.0 if out_ref is k_ref else B_HEAD_K ** -0.5
                    y = y * (lax.rsqrt(jnp.sum(y * y, axis=-1, keepdims=True) + EPS) * scale)
                out_ref[r0:r0 + rc, head * 128:(head + 1) * 128] = y.astype(BF16)

    project(0)
    terms = gate_and_terms()
    yield
    for n in range(n_blocks):
        if n + 1 < n_blocks:
            project(n + 1)
        yield
        conv(n)
        if n == 1:
            decay_sums(*terms)


def _delta_front_kernel(*refs, tm, n_tiles, n_total):
    n_in, n_out = 17, 7
    ins, outs, scr = refs[:n_in], refs[n_in:n_in + n_out], refs[n_in + n_out:]
    gate_ref, w_ref, u_ref, qg_ref, kd_ref, qk_ref, egl_ref = outs
    hn_scr, proj_scr, q_scr, k_scr, v_scr, col_scr, row_scr = scr
    j = pl.program_id(0)

    @pl.when(j == 0)
    def _():
        for ref in (q_scr, k_scr, v_scr, col_scr, row_scr):
            ref[1] = jnp.zeros(ref.shape[1:], ref.dtype)

    cur = j % 2
    prev = 1 - cur
    tile = lax.rem(jnp.minimum(j, n_total - 1), n_tiles)
    proj = _proj_stages(tile, *ins, q_scr.at[cur], k_scr.at[cur], v_scr.at[cur], gate_ref, col_scr.at[cur],
                        row_scr.at[cur], hn_scr, proj_scr, tm=tm, n_tiles=n_tiles)
    n_chunks = tm // CHUNK
    waves = [range(c0, c0 + PREP_WAVE) for c0 in range(0, n_chunks, PREP_WAVE)]
    prep = itertools.chain.from_iterable(
        _prep_stages(q_scr.at[prev], k_scr.at[prev], v_scr.at[prev], col_scr.at[prev], row_scr.at[prev],
                     w_ref, u_ref, qg_ref, kd_ref, qk_ref, egl_ref, chunk_ids=wave) for wave in waves)
    live = {"proj": proj, "prep": prep}
    turn = 0
    while live:
        for name in ("proj", "prep", "prep") if turn % 2 else ("proj", "prep"):
            if name in live and next(live[name], StopIteration) is StopIteration:
                del live[name]
        turn += 1


def _delta_front(x1, gain, w_in, conv_w, a_log, dt_bias):
    b, s, _ = x1.shape
    tm = FRONT_TILE
    assert s % tm == 0
    n_tiles = s // tm
    nc = s // CHUNK
    per = tm // HALO
    nh = s // HALO
    kw = B_HEADS * B_HEAD_K
    wqkv = w_in[:, :N_QKV].astype(BF16)
    wg = w_in[:, N_QKV:N_QKV + kw].astype(BF16)
    wba = w_in[:, N_QKV + kw:].astype(BF16)
    wbat = wba.T
    pad = jnp.zeros((1, 2 * B_HEADS), F32)
    alog = jnp.concatenate([pad, a_log.reshape(1, 2 * B_HEADS).astype(F32)], axis=1)
    dtb = jnp.concatenate([pad, dt_bias.reshape(1, 2 * B_HEADS).astype(F32)], axis=1)
    i = np.arange(CHUNK)
    tril = (i[:, None] >= i[None, :]).astype(np.float32)
    triu = tril.T
    eye = np.eye(tm // CHUNK, dtype=np.float32)
    trilt = jnp.asarray(np.kron(eye, tril.T), BF16)
    triut = jnp.asarray(np.kron(eye, triu.T), BF16)
    tril = jnp.asarray(tril, BF16)
    triu = jnp.asarray(triu, BF16)
    full = lambda a: pl.BlockSpec(a.shape, lambda j_: (0,) * a.ndim, pipeline_mode=pl.Buffered(1))
    n_total = b * n_tiles
    cur = lambda j_: divmod(jnp.minimum(j_, n_total - 1), n_tiles)
    done = lambda j_: divmod(jnp.maximum(j_ - 1, 0), n_tiles)
    both = lambda n: pl.BlockSpec((None, 2, tm, n), lambda j_: (done(j_)[0], 0, done(j_)[1], 0))
    act = jax.ShapeDtypeStruct((b, 2, s, kw), BF16)
    kernel = functools.partial(_delta_front_kernel, tm=tm, n_tiles=n_tiles, n_total=n_total)
    consts = [gain, wqkv, wg, wba, wbat, conv_w.astype(F32), alog, dtb, alog.T, dtb.T, tril, triu, trilt, triut]
    return pl.pallas_call(
        kernel,
        grid=(n_total + 1,),
        in_specs=[
            pl.BlockSpec((None, HALO, D_MODEL), lambda j_: (cur(j_)[0], jnp.maximum(cur(j_)[1] * per - 1, 0), 0)),
            pl.BlockSpec((None, tm, D_MODEL), lambda j_: (cur(j_)[0], cur(j_)[1], 0)),
            pl.BlockSpec((None, HALO, D_MODEL),
                         lambda j_: (cur(j_)[0], jnp.minimum((cur(j_)[1] + 1) * per, nh - 1), 0)),
        ] + [full(a) for a in consts],
        out_specs=[pl.BlockSpec((None, tm, kw), lambda j_: (cur(j_)[0], cur(j_)[1], 0)),
                   both(kw), both(kw), both(kw), both(kw), both(B_HEADS * CHUNK),
                   pl.BlockSpec((None, 2, tm // CHUNK, 1, kw), lambda j_: (done(j_)[0], 0, done(j_)[1], 0, 0))],
        out_shape=[jax.ShapeDtypeStruct((b, s, kw), BF16), act, act, act, act,
                   jax.ShapeDtypeStruct((b, 2, s, B_HEADS * CHUNK), BF16),
                   jax.ShapeDtypeStruct((b, 2, nc, 1, kw), F32)],
        scratch_shapes=[pltpu.VMEM((tm + 2 * HALO, D_MODEL), BF16),
                        pltpu.VMEM((PROJ_RING, tm + 2 * HALO, PROJ_BLOCK), F32),
                        pltpu.VMEM((2, tm, kw), BF16), pltpu.VMEM((2, tm, kw), BF16), pltpu.VMEM((2, tm, kw), BF16),
                        pltpu.VMEM((2, tm, 4 * B_HEADS), F32), pltpu.VMEM((2, 4 * B_HEADS, tm), F32)],
        compiler_params=_params(("arbitrary",)),
        name="delta_front",
    )(x1, x1, x1, *consts)


FRONT_TILE = 256
PREP_WAVE = 4
GROUP = 4
N_GROUPS = B_HEADS // GROUP
PAIR = 2
N_PAIRS = B_HEADS // PAIR


def _block_diag(x, n_blocks):
    r, width = x.shape
    shift = (width // n_blocks).bit_length() - 1
    blk = lax.broadcasted_iota(jnp.int32, (r, width), 1) >> shift
    zero = jnp.zeros_like(x)
    return jnp.concatenate([jnp.where(blk == a, x, zero) for a in range(n_blocks)], axis=0)


def _prep_stages(q_ref, k_ref, v_ref, col_ref, row_ref, w_ref, u_ref, qg_ref, kd_ref, qk_ref, egl_ref, *, chunk_ids):
    width = GROUP * CHUNK

    def row_piece(kind, c, g):
        first = kind * B_HEADS + g * GROUP
        return jnp.concatenate([row_ref[first + a:first + a + 1, c * CHUNK:(c + 1) * CHUNK] for a in range(GROUP)],
                               axis=1)

    ri = lax.broadcasted_iota(jnp.int32, (CHUNK, width), 0)
    cj = lax.broadcasted_iota(jnp.int32, (CHUNK, width), 1) & (CHUNK - 1)
    low_half = lax.broadcasted_iota(jnp.int32, (CHUNK, 128), 1) < CHUNK
    eye = (ri == cj).astype(F32)
    masks = ((ri >= cj, ri > cj, CHUNK - 1), (ri <= cj, ri < cj, 0))

    def side_by_side(parts):
        return jnp.concatenate([jnp.where(low_half, parts[0], parts[1]), jnp.where(low_half, parts[2], parts[3])],
                               axis=1)

    gram, kbd = {}, {}
    for c in chunk_ids:
        rows = slice(c * CHUNK, (c + 1) * CHUNK)
        for g in range(N_GROUPS):
            cols = slice(g * 512, (g + 1) * 512)
            k4 = k_ref[rows, cols]
            kbd[c, g] = _block_diag(k4, GROUP)
            gram[c, g] = _dot_nt(jnp.concatenate([q_ref[rows, cols], k4], axis=0), kbd[c, g])
    yield

    x, p, dq, dk = {}, {}, {}, {}
    is_eye = ri == cj
    for c in chunk_ids:
        rows = slice(c * CHUNK, (c + 1) * CHUNK)
        for d, (incl, strict, last) in enumerate(masks):
            for g in range(N_GROUPS):
                beta_cols, gc_cols, last_rows = [], [], []
                for a in range(GROUP):
                    h = g * GROUP + a
                    idx = d * B_HEADS + h
                    hcols = slice(h * 128, (h + 1) * 128)
                    beta_cols.append(jnp.broadcast_to(col_ref[rows, idx:idx + 1], (CHUNK, 128)))
                    gc_cols.append(jnp.broadcast_to(col_ref[rows, 2 * B_HEADS + idx:2 * B_HEADS + idx + 1],
                                                    (CHUNK, 128)))
                    t_last = c * CHUNK + last
                    g_last = row_ref[2 * B_HEADS + idx:2 * B_HEADS + idx + 1, t_last:t_last + 1]
                    egl_ref[d, c, :, hcols] = jnp.exp(jnp.broadcast_to(g_last, (1, 128)))
                    last_rows.append(jnp.broadcast_to(g_last, (1, CHUNK)))
                lanes = slice(g * width, (g + 1) * width)
                gc_row = row_piece(2 + d, c, g)
                gc_col = side_by_side(gc_cols)
                dq[c, d, g] = jnp.where(is_eye, jnp.exp(gc_col), 0.0).astype(BF16)
                dk[c, d, g] = jnp.where(is_eye, jnp.exp(jnp.concatenate(last_rows, axis=1) - gc_col), 0.0).astype(BF16)
                decay = jnp.where(incl, jnp.exp(jnp.where(incl, gc_col - gc_row, 0.0)), 0.0)
                gm = gram[c, g]
                qk_ref[d, rows, lanes] = (gm[:CHUNK] * decay).astype(BF16)
                low = jnp.where(strict, gm[CHUNK:] * decay * side_by_side(beta_cols), 0.0)
                x[c, d, g] = -low
                p[c, d, g] = eye - low
        if c % 2 == 1:
            yield

    keys = list(x)
    for key in keys:
        xb = x[key].astype(BF16)
        x[key] = _dot(xb, _block_diag(xb, GROUP))
    yield
    squarings = CHUNK.bit_length() - 2
    for _ in range(squarings - 1):
        for key in keys:
            xb = x[key].astype(BF16)
            z = _dot(jnp.concatenate([xb, p[key].astype(BF16)], axis=0), _block_diag(xb, GROUP))
            x[key] = z[:CHUNK]
            p[key] = p[key] + z[CHUNK:]
        yield
    for key in keys:
        p[key] = p[key] + _dot(p[key].astype(BF16), _block_diag(x[key].astype(BF16), GROUP))
    yield

    for c in chunk_ids:
        rows = slice(c * CHUNK, (c + 1) * CHUNK)
        for g in range(N_GROUPS):
            cols = slice(g * 512, (g + 1) * 512)
            tb = [p[c, d, g] * row_piece(d, c, g) for d in range(2)]
            tbg = [tb[d] * jnp.exp(row_piece(2 + d, c, g)) for d in range(2)]
            uu = _dot(jnp.concatenate(tb, axis=0).astype(BF16), _block_diag(v_ref[rows, cols], GROUP))
            k_lhs = jnp.concatenate([jnp.concatenate(tbg, axis=0).astype(BF16), dk[c, 0, g], dk[c, 1, g]], axis=0)
            ww = _dot(k_lhs, kbd[c, g])
            qq = _dot(jnp.concatenate([dq[c, 0, g], dq[c, 1, g]], axis=0), _block_diag(q_ref[rows, cols], GROUP))
            for d in range(2):
                u_ref[d, rows, cols] = uu[d * CHUNK:(d + 1) * CHUNK].astype(BF16)
                w_ref[d, rows, cols] = ww[d * CHUNK:(d + 1) * CHUNK].astype(BF16)
                kd_ref[d, rows, cols] = ww[(2 + d) * CHUNK:(3 + d) * CHUNK].astype(BF16)
                qg_ref[d, rows, cols] = qq[d * CHUNK:(d + 1) * CHUNK].astype(BF16)
    yield


SCAN_CHUNKS = 2


def _delta_scan_kernel(*refs):
    per_dir = 6
    dirs = (refs[:per_dir] + (refs[2 * per_dir],), refs[per_dir:2 * per_dir] + (refs[2 * per_dir + 1],))
    state_ref = refs[2 * per_dir + 2]
    n_batch = dirs[0][0].shape[0]

    @pl.when(pl.program_id(0) == 0)
    def _():
        state_ref[...] = jnp.zeros_like(state_ref)

    tile = PAIR * B_HEAD_K
    units = [(i, d, hp) for i in range(n_batch) for d in range(2) for hp in range(N_PAIRS)]
    slot = lambda i, d, hp: (i * 2 + d) * N_PAIRS + hp
    for step in range(SCAN_CHUNKS):
        sub = (step, SCAN_CHUNKS - 1 - step)
        rows = [slice(c * CHUNK, (c + 1) * CHUNK) for c in sub]
        rs = {}
        for i, d, hp in units:
            w_ref, _, qg_ref = dirs[d][:3]
            cols = slice(hp * tile, (hp + 1) * tile)
            lhs = jnp.concatenate([w_ref[i, rows[d], cols], qg_ref[i, rows[d], cols]], axis=0)
            rs[i, d, hp] = _dot(lhs, state_ref[slot(i, d, hp)].astype(BF16))
        for i, d, hp in units:
            _, u_ref, _, kd_ref, qk_ref, egl_ref, o_ref = dirs[d]
            cols = slice(hp * tile, (hp + 1) * tile)
            r = rs[i, d, hp]
            v_new = (u_ref[i, rows[d], cols].astype(F32) - r[:CHUNK]).astype(BF16)
            qk = qk_ref[i, rows[d], hp * PAIR * CHUNK:(hp + 1) * PAIR * CHUNK]
            o_ref[i, rows[d], cols] = (r[CHUNK:] + _dot(qk, _block_diag(v_new, PAIR))).astype(o_ref.dtype)
            idx = slot(i, d, hp)
            for a in range(PAIR):
                blk = slice(a * B_HEAD_K, (a + 1) * B_HEAD_K)
                hcols = slice(hp * tile + a * B_HEAD_K, hp * tile + (a + 1) * B_HEAD_K)
                update = _dot_tn(kd_ref[i, rows[d], hcols], v_new[:, blk])
                state_ref[idx, blk, blk] = state_ref[idx, blk, blk] * egl_ref[i, sub[d], :, hcols] + update


def _delta_scan(w, u, qg, kd, qk, egl):
    b, _, s, kw = w.shape
    rows = SCAN_CHUNKS * CHUNK
    assert s % rows == 0
    n_steps = s // rows
    chunk_of = (lambda c: c, lambda c: n_steps - 1 - c)
    in_specs = []
    for d in range(2):
        act = lambda n, d=d: pl.BlockSpec((b, None, rows, n), lambda c: (0, d, chunk_of[d](c), 0))
        in_specs += [act(kw), act(kw), act(kw), act(kw), act(B_HEADS * CHUNK),
                     pl.BlockSpec((b, None, SCAN_CHUNKS, 1, kw), lambda c, d=d: (0, d, chunk_of[d](c), 0, 0))]
    out = jax.ShapeDtypeStruct((b, s, kw), BF16)
    out_specs = [pl.BlockSpec((b, rows, kw), lambda c, d=d: (0, chunk_of[d](c), 0)) for d in range(2)]
    args = (w, u, qg, kd, qk, egl)
    return pl.pallas_call(
        _delta_scan_kernel,
        grid=(n_steps,),
        in_specs=in_specs,
        out_specs=out_specs,
        out_shape=[out, out],
        scratch_shapes=[pltpu.VMEM((b * 2 * N_PAIRS, PAIR * B_HEAD_K, PAIR * B_HEAD_V), F32)],
        compiler_params=_params(("arbitrary",)),
        name="delta_scan",
    )(*args, *args)


OUT_TILE = 1024


def _delta_out_kernel(of_ref, ob_ref, gate_ref, x_ref, onorm_ref, w_ref, fnorm_ref, o_ref, og_scr):
    onorm = onorm_ref[...]
    for h in range(B_HEADS):
        cols = slice(h * 128, (h + 1) * 128)
        o = of_ref[:, cols].astype(F32) + ob_ref[:, cols].astype(F32)
        o = o * lax.rsqrt(jnp.mean(o * o, axis=-1, keepdims=True) + EPS) * onorm
        gate = gate_ref[:, cols].astype(F32)
        og_scr[:, cols] = (o * _silu(gate)).astype(BF16)
    y = x_ref[...] + _dot(og_scr[...], w_ref[...])
    o_ref[...] = _rms_rows(y, fnorm_ref[...])


def _delta_out(o_f, o_b, gate, x1, out_norm, w_out, final_norm):
    t = x1.shape[0]
    tm = OUT_TILE
    assert t % tm == 0
    row = pl.BlockSpec((tm, D_MODEL), lambda i: (i, 0))
    vec = lambda n: pl.BlockSpec((1, n), lambda i: (0, 0))
    return pl.pallas_call(
        _delta_out_kernel,
        grid=(t // tm,),
        in_specs=[row, row, row, row, vec(B_HEAD_V), pl.BlockSpec(w_out.shape, lambda i: (0, 0)), vec(D_MODEL)],
        out_specs=row,
        out_shape=jax.ShapeDtypeStruct((t, D_MODEL), F32),
        scratch_shapes=[pltpu.VMEM((tm, D_MODEL), BF16)],
        compiler_params=_params(("parallel",)),
        name="delta_out",
    )(o_f, o_b, gate, x1, out_norm, w_out, final_norm)


def kernel(x, attn_norm, attn_w_in, attn_sink, attn_w_out, delta_norm, delta_w_in, delta_conv, delta_a_log,
           delta_dt_bias, delta_out_norm, delta_w_out, final_norm):
    b, s, d = x.shape
    assert d == D_MODEL
    t = b * s
    q_t, k, v_t, gate_t = _attn_proj(x, attn_norm.reshape(1, d).astype(F32), attn_w_in)
    x1 = _attn_core(x, q_t, k, v_t, gate_t, attn_sink.astype(F32), attn_w_out.astype(BF16))
    dgate, *operators = _delta_front(x1, delta_norm.reshape(1, d).astype(F32), delta_w_in, delta_conv,
                                     delta_a_log, delta_dt_bias)
    o_f, o_b = _delta_scan(*operators)
    out = _delta_out(o_f.reshape(t, d), o_b.reshape(t, d), dgate.reshape(t, d), x1.reshape(t, d),
                     delta_out_norm.reshape(1, B_HEAD_V).astype(F32), delta_w_out.astype(BF16),
                     final_norm.reshape(1, d).astype(F32))
    return out.reshape(b, s, d)
```
